```python
import jax, jax.numpy as jnp
from jax import lax
import numpy as np

D_MODEL = 1024
BATCH = 4
SEQ = 8192
DEPTH = 1
DEC_BATCH = 32
DEC_SEQ = 1
PAST_LEN = 16384
PAGE_SIZE = 128

HEAD_DIM = 64
HEADS_PER_GROUP = 4
GROUPS = ((128, 1), (512, 4), (2048, 16))
N_GROUPS = len(GROUPS)
N_HEADS = HEADS_PER_GROUP * N_GROUPS
ATTN_WIDTH = N_HEADS * HEAD_DIM
ATTN_OUT_WIDTH = HEADS_PER_GROUP * HEAD_DIM
ROT_DIM = HEAD_DIM // 4
ROPE_THETA = 500000.0
C_CONV = D_MODEL
CONV_WIDTH = 31
D_FF = 2816
FFN_CONV_WIDTH = 3
EPS = 1e-6
NEG = -1e30
IN_WIDTH = 2 * C_CONV + 3 * ATTN_WIDTH + 2 * D_MODEL
SPLITS = (C_CONV, 2 * C_CONV, 2 * C_CONV + ATTN_WIDTH, 2 * C_CONV + 2 * ATTN_WIDTH,
          2 * C_CONV + 3 * ATTN_WIDTH, 2 * C_CONV + 3 * ATTN_WIDTH + D_MODEL)

kernel_name = 'hybrid_conformer_conv_dilated_swa_convffn_step'


def rms_norm(x, g):
    xf = x.astype(jnp.float32)
    y = xf * lax.rsqrt(jnp.mean(xf * xf, axis=-1, keepdims=True) + EPS)
    return (y * g.astype(jnp.float32)).astype(x.dtype)


def layer_norm(x, g, b):
    xf = x.astype(jnp.float32)
    mu = jnp.mean(xf, axis=-1, keepdims=True)
    var = jnp.mean(jnp.square(xf - mu), axis=-1, keepdims=True)
    y = (xf - mu) * lax.rsqrt(var + EPS) * g.astype(jnp.float32) + b.astype(jnp.float32)
    return y.astype(x.dtype)


def causal_dwconv(hist, x, w, b):
    xe = jnp.concatenate([hist.astype(x.dtype), x], axis=1)
    y = lax.conv_general_dilated(xe, w[:, None, :].astype(x.dtype), (1,), 'VALID',
                                 dimension_numbers=('NWC', 'WIO', 'NWC'),
                                 feature_group_count=x.shape[-1])
    return y + b.astype(x.dtype), xe[:, -(w.shape[0] - 1):]


def rope(x, pos):
    half = ROT_DIM // 2
    inv = ROPE_THETA ** (-jnp.arange(half, dtype=jnp.float32) / half)
    ang = pos.astype(jnp.float32)[:, None] * inv[None, :]
    cos = jnp.cos(ang)[None, :, None, :]
    sin = jnp.sin(ang)[None, :, None, :]
    xr = x[..., :ROT_DIM].astype(jnp.float32)
    x1, x2 = xr[..., :half], xr[..., half:]
    rot = jnp.concatenate([x1 * cos - x2 * sin, x2 * cos + x1 * sin], axis=-1).astype(x.dtype)
    return jnp.concatenate([rot, x[..., ROT_DIM:]], axis=-1)


def dilated_group_prompt(q, k, v, dilation, n_keys):
    b_sz, s_len, n_h, d_h = q.shape
    m_len = s_len // dilation
    nb = -(-m_len // n_keys)
    mp = nb * n_keys

    def split(t):
        t = t.reshape(b_sz, m_len, dilation, n_h, d_h).transpose(0, 2, 1, 3, 4)
        return jnp.pad(t, ((0, 0), (0, 0), (0, mp - m_len), (0, 0), (0, 0)))

    def band_keys(t):
        t = jnp.pad(split(t), ((0, 0), (0, 0), (n_keys, 0), (0, 0), (0, 0)))
        t = t.reshape(b_sz, dilation, nb + 1, n_keys, n_h, d_h)
        return jnp.concatenate([t[:, :, :-1], t[:, :, 1:]], axis=3)

    qb = split(q).reshape(b_sz, dilation, nb, n_keys, n_h, d_h)
    kb = band_keys(k)
    vb = band_keys(v)
    s = jnp.einsum('brnqhd,brnkhd->brnhqk', qb, kb).astype(jnp.float32)
    qi = jnp.arange(n_keys)[:, None]
    kj = jnp.arange(2 * n_keys)[None, :]
    dist = qi + n_keys - kj
    band = (dist >= 0) & (dist <= n_keys)
    blk = jnp.arange(nb)[:, None, None]
    valid = band[None] & (blk * n_keys + kj[None] - n_keys >= 0)
    s = jnp.where(valid[None, None, :, None], s, NEG)
    mx = jnp.max(s, axis=-1, keepdims=True)
    p = jnp.exp(s - mx)
    den = jnp.sum(p, axis=-1, keepdims=True)
    o = jnp.einsum('brnhqk,brnkhd->brnqhd', p, vb.astype(jnp.float32))
    o = o / jnp.swapaxes(den[..., 0], -1, -2)[..., None]
    lse = jnp.swapaxes(mx[..., 0] + jnp.log(den[..., 0]), -1, -2)
    o = o.reshape(b_sz, dilation, mp, n_h, d_h)[:, :, :m_len].transpose(0, 2, 1, 3, 4)
    lse = lse.reshape(b_sz, dilation, mp, n_h)[:, :, :m_len].transpose(0, 2, 1, 3)
    return o.reshape(b_sz, s_len, n_h, d_h), lse.reshape(b_sz, s_len, n_h)


def dilated_group_sample(q, k_ext, v_ext, dilation, n_keys, hist_len, pos0):
    t_len = q.shape[1]
    t = jnp.arange(t_len)[:, None]
    back = jnp.arange(n_keys + 1)[None, :] * dilation
    idx = hist_len + t - back
    valid = (idx >= 0) & (pos0 + t - back >= 0)
    idx = jnp.maximum(idx, 0)
    kg = jnp.take(k_ext, idx, axis=1)
    vg = jnp.take(v_ext, idx, axis=1)
    s = jnp.einsum('bthd,btjhd->bthj', q, kg).astype(jnp.float32)
    s = jnp.where(valid[None, :, None, :], s, NEG)
    mx = jnp.max(s, axis=-1, keepdims=True)
    p = jnp.exp(s - mx)
    den = jnp.sum(p, axis=-1, keepdims=True)
    o = jnp.einsum('bthj,btjhd->bthd', p, vg.astype(jnp.float32)) / den
    return o, (mx + jnp.log(den))[..., 0]


def _layer(x, pos0, hist_conv, hist_kv, hist_ffn, prompt, g_mix, w_in, w_dw, b_dw, ln_g, ln_b,
           w_conv_out, w_attn_out, w_out, g_ffn, w_up, w_fdw, b_fdw, w_down):
    n_seq, t_len, _ = x.shape
    pos = pos0 + jnp.arange(t_len, dtype=jnp.int32)
    h = rms_norm(x, g_mix)
    z = h @ w_in
    a_lin, a_gate, q, k, v, g_a, g_b = jnp.split(z, SPLITS, axis=-1)
    u = a_lin * jax.nn.sigmoid(a_gate)
    c, new_conv = causal_dwconv(hist_conv, u, w_dw, b_dw)
    out_a = jax.nn.silu(layer_norm(c, ln_g, ln_b)) @ w_conv_out
    q = rope(q.reshape(n_seq, t_len, N_HEADS, HEAD_DIM), pos) * (HEAD_DIM ** -0.5)
    k = rope(k.reshape(n_seq, t_len, N_HEADS, HEAD_DIM), pos)
    v = v.reshape(n_seq, t_len, N_HEADS, HEAD_DIM)
    outs, lses, new_kv = [], [], []
    for gi, (window, dil) in enumerate(GROUPS):
        hs = slice(gi * HEADS_PER_GROUP, (gi + 1) * HEADS_PER_GROUP)
        qg, kg, vg = q[:, :, hs], k[:, :, hs], v[:, :, hs]
        n_keys = window // dil
        if prompt:
            o, l = dilated_group_prompt(qg, kg, vg, dil, n_keys)
            ke, ve = kg, vg
        else:
            kh, vh = hist_kv[gi]
            ke = jnp.concatenate([kh.astype(kg.dtype), kg], axis=1)
            ve = jnp.concatenate([vh.astype(vg.dtype), vg], axis=1)
            o, l = dilated_group_sample(qg, ke, ve, dil, n_keys, kh.shape[1], pos0)
        keep = min(window, pos0 + t_len)
        new_kv.append((ke[:, -keep:], ve[:, -keep:]))
        outs.append(o)
        lses.append(l)
    w_grp = jax.nn.softmax(jnp.stack(lses, axis=0), axis=0)
    o = jnp.sum(w_grp[..., None] * jnp.stack(outs, axis=0), axis=0)
    out_b = o.reshape(n_seq, t_len, ATTN_OUT_WIDTH).astype(x.dtype) @ w_attn_out
    mix = jax.nn.sigmoid(g_a) * out_a + jax.nn.sigmoid(g_b) * out_b
    x = x + mix @ w_out
    h2 = rms_norm(x, g_ffn)
    up, new_ffn = causal_dwconv(hist_ffn, h2 @ w_up, w_fdw, b_fdw)
    gate, val = jnp.split(up, 2, axis=-1)
    x = x + (jax.nn.silu(gate) * val) @ w_down
    return x, new_conv, new_kv, new_ffn


def setup_inputs(seed: int = 0) -> dict:
    key = jax.random.key(seed)
    ks = iter(jax.random.split(key, 32))

    def nrm(shape, scale):
        return scale * jax.random.normal(next(ks), shape, jnp.float32)

    L = DEPTH
    w0, w1, w2 = GROUPS[0][0], GROUPS[1][0], GROUPS[2][0]
    kvs = lambda w: (L, DEC_BATCH, min(w, PAST_LEN), HEADS_PER_GROUP, HEAD_DIM)
    return {
        'x_prompt': nrm((BATCH, SEQ, D_MODEL), 1.0),
        'x_sample': nrm((DEC_BATCH, DEC_SEQ, D_MODEL), 1.0),
        'state_conv': nrm((L, DEC_BATCH, CONV_WIDTH - 1, C_CONV), 0.5),
        'cache_k_w128': nrm(kvs(w0), 1.0),
        'cache_v_w128': nrm(kvs(w0), 1.0),
        'cache_k_w512': nrm(kvs(w1), 1.0),
        'cache_v_w512': nrm(kvs(w1), 1.0),
        'cache_k_w2048': nrm(kvs(w2), 1.0),
        'cache_v_w2048': nrm(kvs(w2), 1.0),
        'state_ffn_conv': nrm((L, DEC_BATCH, FFN_CONV_WIDTH - 1, 2 * D_FF), 1.0),
        'g_mix': 1.0 + nrm((L, D_MODEL), 0.05),
        'w_in': nrm((L, D_MODEL, IN_WIDTH), D_MODEL ** -0.5),
        'w_dw': nrm((L, CONV_WIDTH, C_CONV), CONV_WIDTH ** -0.5),
        'b_dw': nrm((L, C_CONV), 0.02),
        'ln_g': 1.0 + nrm((L, C_CONV), 0.05),
        'ln_b': nrm((L, C_CONV), 0.02),
        'w_conv_out': nrm((L, C_CONV, D_MODEL), C_CONV ** -0.5),
        'w_attn_out': nrm((L, ATTN_OUT_WIDTH, D_MODEL), ATTN_OUT_WIDTH ** -0.5),
        'w_out': nrm((L, D_MODEL, D_MODEL), D_MODEL ** -0.5),
        'g_ffn': 1.0 + nrm((L, D_MODEL), 0.05),
        'w_up': nrm((L, D_MODEL, 2 * D_FF), D_MODEL ** -0.5),
        'w_fdw': nrm((L, FFN_CONV_WIDTH, 2 * D_FF), FFN_CONV_WIDTH ** -0.5),
        'b_fdw': nrm((L, 2 * D_FF), 0.02),
        'w_down': nrm((L, D_FF, D_MODEL), D_FF ** -0.5),
        'g_final': 1.0 + nrm((D_MODEL,), 0.05),
    }


def reference(x_prompt, x_sample, state_conv, cache_k_w128, cache_v_w128, cache_k_w512, cache_v_w512,
              cache_k_w2048, cache_v_w2048, state_ffn_conv, g_mix, w_in, w_dw, b_dw, ln_g, ln_b,
              w_conv_out, w_attn_out, w_out, g_ffn, w_up, w_fdw, b_fdw, w_down, g_final):
    xp, xs = x_prompt, x_sample
    cp, cs, kvp, kvs, fp, fs = [], [], [], [], [], []
    for l in range(DEPTH):
        w = (g_mix[l], w_in[l], w_dw[l], b_dw[l], ln_g[l], ln_b[l], w_conv_out[l], w_attn_out[l],
             w_out[l], g_ffn[l], w_up[l], w_fdw[l], b_fdw[l], w_down[l])
        zc = jnp.zeros((xp.shape[0], CONV_WIDTH - 1, C_CONV), xp.dtype)
        zf = jnp.zeros((xp.shape[0], FFN_CONV_WIDTH - 1, 2 * D_FF), xp.dtype)
        xp, c1, kv1, f1 = _layer(xp, 0, zc, None, zf, True, *w)
        hist = ((cache_k_w128[l], cache_v_w128[l]), (cache_k_w512[l], cache_v_w512[l]),
                (cache_k_w2048[l], cache_v_w2048[l]))
        xs, c2, kv2, f2 = _layer(xs, PAST_LEN, state_conv[l], hist, state_ffn_conv[l], False, *w)
        cp.append(c1); cs.append(c2); kvp.append(kv1); kvs.append(kv2); fp.append(f1); fs.append(f2)

    def st(lst, gi, j):
        return jnp.stack([e[gi][j] for e in lst], axis=0)

    y_prompt = rms_norm(xp, g_final)
    y_sample = rms_norm(xs, g_final)
    return (y_prompt, y_sample,
            jnp.stack(cp, axis=0), jnp.stack(cs, axis=0),
            st(kvp, 0, 0), st(kvp, 0, 1), st(kvs, 0, 0), st(kvs, 0, 1),
            st(kvp, 1, 0), st(kvp, 1, 1), st(kvs, 1, 0), st(kvs, 1, 1),
            st(kvp, 2, 0), st(kvp, 2, 1), st(kvs, 2, 0), st(kvs, 2, 1),
            jnp.stack(fp, axis=0), jnp.stack(fs, axis=0))
```

```python
import functools

import jax
import jax.numpy as jnp
from jax import lax
from jax.experimental import pallas as pl
from jax.experimental.pallas import tpu as pltpu

F32 = jnp.float32
BF16 = jnp.bfloat16

D_MODEL = 1024
HEAD_DIM = 64
HEADS_PER_GROUP = 4
GROUP_WIDTH = HEADS_PER_GROUP * HEAD_DIM
GROUPS = ((128, 1), (512, 4), (2048, 16))
N_GROUPS = len(GROUPS)
N_KEYS = 128
ATTN_WIDTH = N_GROUPS * GROUP_WIDTH
ROT_DIM = HEAD_DIM // 4
ROT_HALF = ROT_DIM // 2
ROPE_THETA = 500000.0
C_CONV = D_MODEL
CONV_WIDTH = 31
CONV_HALO = 32
D_FF = 2816
FFN_CONV_WIDTH = 3
FFN_HALO = 8
EPS = 1e-6
NEG = -1e30
PAST_LEN = 16384
IN_WIDTH = 2 * C_CONV + 3 * ATTN_WIDTH + 2 * D_MODEL
COL_Q = 2 * C_CONV
COL_K = COL_Q + ATTN_WIDTH
COL_V = COL_K + ATTN_WIDTH
COL_G = COL_V + ATTN_WIDTH

LANES = 128
QKV_BLOCKS = ATTN_WIDTH // LANES
ROW_TILE = 512
ATTN_TILE = 2048
FF_CHUNK = 256
CONV_ROWS = 32
MERGE_ROWS = 256
SAMPLE_BLOCK = 4
VMEM_LIMIT = 56 * 1024 * 1024


def _params(n_axes):
  return pltpu.CompilerParams(
      dimension_semantics=("arbitrary",) * n_axes, vmem_limit_bytes=VMEM_LIMIT)


def _full(shape):
  nd = len(shape)
  return pl.BlockSpec(shape, lambda *_: (0,) * nd)


def _sigmoid(x):
  return 1.0 / (1.0 + jnp.exp(-x))


def _rms_norm(x, g):
  return x * lax.rsqrt(jnp.mean(x * x, axis=-1, keepdims=True) + EPS) * g


def _rope_tables(pos):
  inv = ROPE_THETA ** (-jnp.arange(ROT_HALF, dtype=F32) / ROT_HALF)
  ang = pos.astype(F32)[:, None] * inv[None, :]
  cos, sin = jnp.cos(ang), jnp.sin(ang)
  n = pos.shape[0]
  one = jnp.ones((n, HEAD_DIM - ROT_DIM), F32)
  zero = jnp.zeros((n, HEAD_DIM - ROT_DIM), F32)
  z8 = jnp.zeros((n, ROT_HALF), F32)
  reps = LANES // HEAD_DIM
  c = jnp.tile(jnp.concatenate([cos, cos, one], axis=1), (1, reps))
  sa = jnp.tile(jnp.concatenate([-sin, z8, zero], axis=1), (1, reps))
  sb = jnp.tile(jnp.concatenate([z8, sin, zero], axis=1), (1, reps))
  return c, sa, sb


def _inproj_body(x_ref, g_ref, w_ref, cos_ref, sa_ref, sb_ref, u_ref, gate_ref, *rest, tm, dilated):
  h = _rms_norm(x_ref[...], g_ref[...]).astype(BF16)
  a = jnp.dot(h, w_ref[:, 0:2 * C_CONV], preferred_element_type=F32)
  u_ref[...] = a[:, :C_CONV] * _sigmoid(a[:, C_CONV:])
  gate_ref[...] = jnp.dot(h, w_ref[:, COL_G:IN_WIDTH], preferred_element_type=F32).astype(BF16)

  cos, sa, sb = cos_ref[...], sa_ref[...], sb_ref[...]

  def rope(z):
    return z * cos + pltpu.roll(z, LANES - ROT_HALF, 1) * sa + pltpu.roll(z, ROT_HALF, 1) * sb

  if dilated:
    qkv_refs, tail_refs, zs_ref = rest[:N_GROUPS], rest[N_GROUPS:3 * N_GROUPS], rest[3 * N_GROUPS]
  else:
    (zs_ref,) = rest
  zq = jnp.dot(h, w_ref[:, COL_Q:COL_K], preferred_element_type=F32)
  zk = jnp.dot(h, w_ref[:, COL_K:COL_V], preferred_element_type=F32)
  zv = jnp.dot(h, w_ref[:, COL_V:COL_G], preferred_element_type=F32)
  for s in range(QKV_BLOCKS):
    cols = slice(s * LANES, (s + 1) * LANES)
    zs_ref[s] = rope(zq[:, cols]) * (HEAD_DIM ** -0.5)
    zs_ref[QKV_BLOCKS + s] = rope(zk[:, cols])
    zs_ref[2 * QKV_BLOCKS + s] = zv[:, cols]
  if not dilated:
    return

  group_blocks = GROUP_WIDTH // LANES
  for gi, (window, dil) in enumerate(GROUPS):
    rows = tm // dil
    tw = min(tm, window)
    for part in range(3):
      for half in range(group_blocks):
        blk = part * QKV_BLOCKS + gi * group_blocks + half
        dst = slice(part * GROUP_WIDTH + half * LANES, part * GROUP_WIDTH + (half + 1) * LANES)
        for r in range(dil):
          qkv_refs[gi][r, :, dst] = zs_ref[blk, pl.ds(r, rows, stride=dil), :].astype(BF16)
        if part > 0:
          tail_refs[2 * gi + part - 1][:, half * LANES:(half + 1) * LANES] = zs_ref[blk, tm - tw:tm, :]


def _inproj_prompt(x2, g_mix, w_in, tables, n_seq, t_len):
  tm, ta = ROW_TILE, ATTN_TILE
  nt = t_len // tm
  m = n_seq * t_len
  row = lambda i: (i, 0)
  tab = lambda i: (i % nt, 0)
  in_specs = [pl.BlockSpec((tm, D_MODEL), row), _full((1, D_MODEL)), _full((D_MODEL, IN_WIDTH)),
              pl.BlockSpec((tm, LANES), tab), pl.BlockSpec((tm, LANES), tab), pl.BlockSpec((tm, LANES), tab)]
  out_shape = [jax.ShapeDtypeStruct((m, C_CONV), F32), jax.ShapeDtypeStruct((m, 2 * D_MODEL), BF16)]
  out_specs = [pl.BlockSpec((tm, C_CONV), row), pl.BlockSpec((tm, 2 * D_MODEL), row)]
  sub = ta // tm
  for window, dil in GROUPS:
    out_shape.append(jax.ShapeDtypeStruct((n_seq, t_len // ta, dil, ta // dil, 3 * GROUP_WIDTH), BF16))
    out_specs.append(pl.BlockSpec(
        (None, None, dil, tm // dil, 3 * GROUP_WIDTH),
        lambda i: (i // nt, (i % nt) // sub, 0, (i % nt) % sub, 0)))
  for window, dil in GROUPS:
    tw = min(tm, window)
    first = (t_len - window) // tm
    for _ in range(2):
      out_shape.append(jax.ShapeDtypeStruct((n_seq, window, GROUP_WIDTH), F32))
      out_specs.append(pl.BlockSpec(
          (None, tw, GROUP_WIDTH),
          functools.partial(lambda i, first: (i // nt, jnp.maximum(i % nt - first, 0), 0), first=first)))
  return pl.pallas_call(
      functools.partial(_inproj_body, tm=tm, dilated=True),
      grid=(m // tm,), in_specs=in_specs, out_specs=out_specs, out_shape=out_shape,
      scratch_shapes=[pltpu.VMEM((3 * QKV_BLOCKS, tm, LANES), F32)],
      compiler_params=_params(1), name="inproj_prompt",
  )(x2, g_mix, w_in, *tables)


def _inproj_sample(x2, g_mix, w_in, tables):
  m = x2.shape[0]
  out_shape = [jax.ShapeDtypeStruct((m, C_CONV), F32), jax.ShapeDtypeStruct((m, 2 * D_MODEL), BF16),
               jax.ShapeDtypeStruct((3 * QKV_BLOCKS, m, LANES), F32)]
  return pl.pallas_call(
      functools.partial(_inproj_body, tm=m, dilated=False),
      grid=(1,),
      in_specs=[_full((m, D_MODEL)), _full((1, D_MODEL)), _full((D_MODEL, IN_WIDTH)),
                _full((m, LANES)), _full((m, LANES)), _full((m, LANES))],
      out_specs=[_full(s.shape) for s in out_shape], out_shape=out_shape,
      compiler_params=_params(1), name="inproj_sample",
  )(x2, g_mix, w_in, *tables)


def _conv_post(c, ga, lng_ref, lnb_ref, wco_ref):
  mu = jnp.mean(c, axis=-1, keepdims=True)
  cc = c - mu
  var = jnp.mean(cc * cc, axis=-1, keepdims=True)
  y = cc * lax.rsqrt(var + EPS) * lng_ref[...] + lnb_ref[...]
  s = y * _sigmoid(y)
  out_a = jnp.dot(s.astype(BF16), wco_ref[...], preferred_element_type=F32)
  return (_sigmoid(ga.astype(F32)) * out_a).astype(BF16)


def _conv_prompt_body(u_ref, ga_ref, wdw_ref, bdw_ref, lng_ref, lnb_ref, wco_ref, out_ref, state_ref,
                      xe_ref, c_ref, *, tm, nt):
  it = pl.program_id(0) % nt

  @pl.when(it == 0)
  def _():
    xe_ref[0:CONV_HALO, :] = jnp.zeros((CONV_HALO, C_CONV), F32)

  xe_ref[CONV_HALO:CONV_HALO + tm, :] = u_ref[...]
  lead = CONV_HALO - (CONV_WIDTH - 1)

  for cb in range(C_CONV // LANES):
    cols = slice(cb * LANES, (cb + 1) * LANES)
    taps = [wdw_ref[j:j + 1, cols] for j in range(CONV_WIDTH)]
    bias = bdw_ref[:, cols]

    def rows_body(rb, carry, cols=cols, taps=taps, bias=bias):
      r0 = pl.multiple_of(rb * CONV_ROWS, CONV_ROWS)
      win = xe_ref[pl.ds(r0, CONV_ROWS + CONV_HALO), cols]
      acc = jnp.broadcast_to(bias, (CONV_ROWS, LANES))
      for j in range(CONV_WIDTH):
        acc = acc + taps[j] * win[lead + j:lead + j + CONV_ROWS]
      c_ref[pl.ds(r0, CONV_ROWS), cols] = acc
      return carry

    lax.fori_loop(0, tm // CONV_ROWS, rows_body, 0)

  out_ref[...] = _conv_post(c_ref[...], ga_ref[...], lng_ref, lnb_ref, wco_ref)
  state_ref[...] = xe_ref[CONV_HALO + tm - (CONV_WIDTH - 1):CONV_HALO + tm, :]
  xe_ref[0:CONV_HALO, :] = xe_ref[tm:tm + CONV_HALO, :]


def _conv_prompt(u2, gate2, w_dw, b_dw, ln_g, ln_b, w_co, n_seq, t_len):
  tm = ROW_TILE
  nt = t_len // tm
  m = n_seq * t_len
  row = lambda i: (i, 0)
  return pl.pallas_call(
      functools.partial(_conv_prompt_body, tm=tm, nt=nt),
      grid=(m // tm,),
      in_specs=[pl.BlockSpec((tm, C_CONV), row), pl.BlockSpec((tm, D_MODEL), row),
                _full((CONV_WIDTH, C_CONV)), _full((1, C_CONV)), _full((1, C_CONV)), _full((1, C_CONV)),
                _full((C_CONV, D_MODEL))],
      out_specs=[pl.BlockSpec((tm, D_MODEL), row),
                 pl.BlockSpec((None, CONV_WIDTH - 1, C_CONV), lambda i: (i // nt, 0, 0))],
      out_shape=[jax.ShapeDtypeStruct((m, D_MODEL), BF16),
                 jax.ShapeDtypeStruct((n_seq, CONV_WIDTH - 1, C_CONV), F32)],
      scratch_shapes=[pltpu.VMEM((CONV_HALO + tm, C_CONV), F32), pltpu.VMEM((tm, C_CONV), F32)],
      compiler_params=_params(1), name="conv_prompt",
  )(u2, gate2, w_dw, b_dw, ln_g, ln_b, w_co)


def _conv_sample_body(u_ref, ga_ref, st_ref, wdw_ref, bdw_ref, lng_ref, lnb_ref, wco_ref, out_ref,
                      state_ref):
  n = u_ref.shape[0]
  u = u_ref[...]
  acc = jnp.broadcast_to(bdw_ref[...], (n, C_CONV)) + wdw_ref[CONV_WIDTH - 1:CONV_WIDTH, :] * u
  for j in range(CONV_WIDTH - 1):
    acc = acc + wdw_ref[j:j + 1, :] * st_ref[:, j, :]
  out_ref[...] = _conv_post(acc, ga_ref[...], lng_ref, lnb_ref, wco_ref)
  for j in range(CONV_WIDTH - 2):
    state_ref[:, j, :] = st_ref[:, j + 1, :]
  state_ref[:, CONV_WIDTH - 2, :] = u


def _conv_sample(u2, gate2, state, w_dw, b_dw, ln_g, ln_b, w_co):
  n = u2.shape[0]
  out_shape = [jax.ShapeDtypeStruct((n, D_MODEL), BF16),
               jax.ShapeDtypeStruct((n, CONV_WIDTH - 1, C_CONV), F32)]
  return pl.pallas_call(
      _conv_sample_body, grid=(1,),
      in_specs=[_full((n, C_CONV)), pl.BlockSpec((n, D_MODEL), lambda i: (0, 0)),
                _full((n, CONV_WIDTH - 1, C_CONV)),
                _full((CONV_WIDTH, C_CONV)), _full((1, C_CONV)), _full((1, C_CONV)), _full((1, C_CONV)),
                _full((C_CONV, D_MODEL))],
      out_specs=[_full(s.shape) for s in out_shape], out_shape=out_shape,
      compiler_params=_params(1), name="conv_sample",
  )(u2, gate2, state, w_dw, b_dw, ln_g, ln_b, w_co)


def _attn_prompt_body(qkv0_ref, qkv1_ref, qkv2_ref, o_ref, ext0_ref, ext1_ref, ext2_ref, ores_ref,
                      lres_ref, onat_ref, lnat_ref, *, ta):
  first_tile = pl.program_id(1) == 0
  qi = lax.broadcasted_iota(jnp.int32, (N_KEYS, 2 * N_KEYS), 0)
  kj = lax.broadcasted_iota(jnp.int32, (N_KEYS, 2 * N_KEYS), 1)
  band = (kj >= qi) & (kj <= qi + N_KEYS)
  lane_head = lax.broadcasted_iota(jnp.int32, (N_KEYS, GROUP_WIDTH), 1) // HEAD_DIM
  head_masks = [lane_head == h for h in range(HEADS_PER_GROUP)]

  for gi, (qkv_ref, ext_ref) in enumerate(((qkv0_ref, ext0_ref), (qkv1_ref, ext1_ref),
                                           (qkv2_ref, ext2_ref))):
    dil = GROUPS[gi][1]
    rows = ta // dil
    nsb = rows // N_KEYS
    @pl.when(first_tile)
    def _(ext_ref=ext_ref, dil=dil):
      ext_ref[:, 0:N_KEYS, :] = jnp.zeros((dil, N_KEYS, 2 * GROUP_WIDTH), BF16)

    for r in range(dil):
      ext_ref[r, N_KEYS:N_KEYS + rows, :] = qkv_ref[r, :, GROUP_WIDTH:3 * GROUP_WIDTH]

    def unit(u, carry, qkv_ref=qkv_ref, ext_ref=ext_ref, nsb=nsb):
      r = u // nsb
      sb = u % nsb
      q0 = pl.multiple_of(sb * N_KEYS, N_KEYS)
      q = qkv_ref[r, pl.ds(q0, N_KEYS), 0:GROUP_WIDTH]
      kv = ext_ref[r, pl.ds(q0, 2 * N_KEYS), :]
      k, v = kv[:, :GROUP_WIDTH], kv[:, GROUP_WIDTH:]
      valid = band & jnp.logical_not(first_tile & (sb == 0) & (kj < N_KEYS))
      o_all = jnp.zeros((N_KEYS, GROUP_WIDTH), F32)
      l_all = jnp.zeros((N_KEYS, GROUP_WIDTH), F32)
      for h in range(HEADS_PER_GROUP):
        qh = jnp.where(head_masks[h], q, jnp.zeros_like(q))
        s = lax.dot_general(qh, k, (((1,), (1,)), ((), ())), preferred_element_type=F32)
        s = jnp.where(valid, s, NEG)
        mx = jnp.max(s, axis=-1, keepdims=True)
        p = jnp.exp(s - mx)
        den = jnp.sum(p, axis=-1, keepdims=True)
        pv = jnp.dot(p.astype(BF16), v, preferred_element_type=F32)
        o_all = jnp.where(head_masks[h], pv / den, o_all)
        l_all = jnp.where(head_masks[h], mx + jnp.log(den), l_all)
      o0 = pl.multiple_of(u * N_KEYS, N_KEYS)
      ores_ref[pl.ds(o0, N_KEYS), :] = o_all
      lres_ref[pl.ds(o0, N_KEYS), :] = l_all
      return carry

    lax.fori_loop(0, ta // N_KEYS, unit, 0)

    for r in range(dil):
      ext_ref[r, 0:N_KEYS, :] = ext_ref[r, rows:rows + N_KEYS, :]
      for half in range(GROUP_WIDTH // LANES):
        cols = slice(half * LANES, (half + 1) * LANES)
        dst = pl.ds(r, rows, stride=dil)
        onat_ref[gi, half, dst, :] = ores_ref[r * rows:(r + 1) * rows, cols]
        lnat_ref[gi, half, dst, :] = lres_ref[r * rows:(r + 1) * rows, cols]

  def merge(c, carry):
    rs = pl.ds(pl.multiple_of(c * MERGE_ROWS, MERGE_ROWS), MERGE_ROWS)
    for half in range(GROUP_WIDTH // LANES):
      l0, l1, l2 = lnat_ref[0, half, rs, :], lnat_ref[1, half, rs, :], lnat_ref[2, half, rs, :]
      lm = jnp.maximum(jnp.maximum(l0, l1), l2)
      w0, w1, w2 = jnp.exp(l0 - lm), jnp.exp(l1 - lm), jnp.exp(l2 - lm)
      o = (w0 * onat_ref[0, half, rs, :] + w1 * onat_ref[1, half, rs, :]
           + w2 * onat_ref[2, half, rs, :]) / (w0 + w1 + w2)
      o_ref[rs, half * LANES:(half + 1) * LANES] = o.astype(BF16)
    return carry

  lax.fori_loop(0, ta // MERGE_ROWS, merge, 0)


def _attn_prompt(qkvs, n_seq, t_len):
  ta = ATTN_TILE
  in_specs, scratch = [], []
  for window, dil in GROUPS:
    in_specs.append(pl.BlockSpec((None, None, dil, ta // dil, 3 * GROUP_WIDTH),
                                 lambda b, t: (b, t, 0, 0, 0)))
    scratch.append(pltpu.VMEM((dil, N_KEYS + ta // dil, 2 * GROUP_WIDTH), BF16))
  scratch += [pltpu.VMEM((ta, GROUP_WIDTH), F32), pltpu.VMEM((ta, GROUP_WIDTH), F32),
              pltpu.VMEM((N_GROUPS, GROUP_WIDTH // LANES, ta, LANES), F32),
              pltpu.VMEM((N_GROUPS, GROUP_WIDTH // LANES, ta, LANES), F32)]
  return pl.pallas_call(
      functools.partial(_attn_prompt_body, ta=ta),
      grid=(n_seq, t_len // ta), in_specs=in_specs,
      out_specs=pl.BlockSpec((ta, GROUP_WIDTH), lambda b, t, nta=t_len // ta: (b * nta + t, 0)),
      out_shape=jax.ShapeDtypeStruct((n_seq * t_len, GROUP_WIDTH), BF16),
      scratch_shapes=scratch, compiler_params=_params(2), name="attn_prompt",
  )(*qkvs)


def _attn_sample_body(q_ref, kn_ref, vn_ref, k0_ref, v0_ref, k1_ref, v1_ref, k2_ref, v2_ref, o_ref):
  caches = ((k0_ref, v0_ref), (k1_ref, v1_ref), (k2_ref, v2_ref))
  for bi in range(q_ref.shape[0]):
    outs, lses = [], []
    for gi in range(N_GROUPS):
      k_ref, v_ref = caches[gi]
      q = q_ref[bi, gi]
      s = jnp.sum(k_ref[bi] * q[None], axis=-1, keepdims=True)
      s_new = jnp.sum(kn_ref[bi, gi] * q, axis=-1, keepdims=True)
      mx = jnp.maximum(jnp.max(s, axis=0), s_new)
      p = jnp.exp(s - mx[None])
      p_new = jnp.exp(s_new - mx)
      den = jnp.sum(p, axis=0) + p_new
      acc = jnp.sum(p * v_ref[bi], axis=0) + p_new * vn_ref[bi, gi]
      outs.append(acc / den)
      lses.append(mx + jnp.log(den))
    lm = jnp.maximum(jnp.maximum(lses[0], lses[1]), lses[2])
    ws = [jnp.exp(l - lm) for l in lses]
    o_ref[bi] = (ws[0] * outs[0] + ws[1] * outs[1] + ws[2] * outs[2]) / (ws[0] + ws[1] + ws[2])


def _attn_sample(q4, kn4, vn4, caches):
  n = q4.shape[0]
  bb = SAMPLE_BLOCK
  small = pl.BlockSpec((bb, N_GROUPS, HEADS_PER_GROUP, HEAD_DIM), lambda i: (i, 0, 0, 0))
  in_specs = [small, small, small]
  args = []
  for (window, dil), (kc, vc) in zip(GROUPS, caches):
    for c in (kc, vc):
      args.append(c.reshape(n, N_KEYS, dil, HEADS_PER_GROUP, HEAD_DIM))
      in_specs.append(pl.BlockSpec((bb, N_KEYS, None, HEADS_PER_GROUP, HEAD_DIM),
                                   lambda i: (i, 0, 0, 0, 0)))
  return pl.pallas_call(
      _attn_sample_body, grid=(n // bb,), in_specs=in_specs,
      out_specs=pl.BlockSpec((bb, HEADS_PER_GROUP, HEAD_DIM), lambda i: (i, 0, 0)),
      out_shape=jax.ShapeDtypeStruct((n, HEADS_PER_GROUP, HEAD_DIM), F32),
      compiler_params=_params(1), name="attn_sample",
  )(q4, kn4, vn4, *args)


def _cache_roll_body(*refs):
  n_c = 2 * N_GROUPS
  olds, news, outs, sem = refs[:n_c], refs[n_c:2 * n_c], refs[2 * n_c:3 * n_c], refs[3 * n_c]
  copies = []
  for i in range(n_c):
    w = olds[i].shape[1]
    copies.append(pltpu.make_async_copy(olds[i].at[:, pl.ds(1, w - 1)], outs[i].at[:, pl.ds(0, w - 1)],
                                        sem.at[2 * i]))
    copies.append(pltpu.make_async_copy(news[i], outs[i].at[:, pl.ds(w - 1, 1)], sem.at[2 * i + 1]))
  for c in copies:
    c.start()
  for c in copies:
    c.wait()


def _cache_roll(olds, news):
  any_spec = pl.BlockSpec(memory_space=pl.ANY)
  return pl.pallas_call(
      _cache_roll_body,
      in_specs=[any_spec] * (2 * len(olds)), out_specs=[any_spec] * len(olds),
      out_shape=[jax.ShapeDtypeStruct(o.shape, o.dtype) for o in olds],
      scratch_shapes=[pltpu.SemaphoreType.DMA((2 * len(olds),))],
      name="cache_roll",
  )(*olds, *news)


def _ffn_body(x_ref, ga_ref, gb_ref, o_ref, wao_ref, wo_ref, gffn_ref, wup_ref, wfdw_ref, bfdw_ref,
              wdn_ref, gfin_ref, *rest, tm, nt, cached):
  if cached:
    hist_ref, y_ref, state_ref = rest
  else:
    y_ref, state_ref, upx_ref = rest
  out_b = jnp.dot(o_ref[...], wao_ref[...], preferred_element_type=F32)
  mix = ga_ref[...].astype(F32) + _sigmoid(gb_ref[...].astype(F32)) * out_b
  x1 = x_ref[...] + jnp.dot(mix.astype(BF16), wo_ref[...], preferred_element_type=F32)
  h2 = _rms_norm(x1, gffn_ref[...]).astype(BF16)

  if not cached:
    @pl.when(pl.program_id(0) % nt == 0)
    def _():
      upx_ref[0:FFN_HALO, :] = jnp.zeros((FFN_HALO, 2 * D_FF), F32)

  acc = x1
  for c in range(D_FF // FF_CHUNK):
    halves = []
    for base in (0, D_FF):
      cols = slice(base + c * FF_CHUNK, base + (c + 1) * FF_CHUNK)
      up = jnp.dot(h2, wup_ref[:, cols], preferred_element_type=F32)
      if cached:
        prev2, prev1 = hist_ref[:, 0, cols], hist_ref[:, 1, cols]
        state_ref[:, 0, cols] = prev1
        state_ref[:, 1, cols] = up
      else:
        upx_ref[FFN_HALO:FFN_HALO + tm, cols] = up
        prev2 = upx_ref[FFN_HALO - 2:FFN_HALO - 2 + tm, cols]
        prev1 = upx_ref[FFN_HALO - 1:FFN_HALO - 1 + tm, cols]
      halves.append(wfdw_ref[0:1, cols] * prev2 + wfdw_ref[1:2, cols] * prev1
                    + wfdw_ref[2:3, cols] * up + bfdw_ref[:, cols])
    gate, val = halves
    act = (gate * _sigmoid(gate) * val).astype(BF16)
    acc = acc + jnp.dot(act, wdn_ref[c * FF_CHUNK:(c + 1) * FF_CHUNK, :], preferred_element_type=F32)
  y_ref[...] = _rms_norm(acc, gfin_ref[...])

  if not cached:
    state_ref[...] = upx_ref[FFN_HALO + tm - (FFN_CONV_WIDTH - 1):FFN_HALO + tm, :]
    upx_ref[0:FFN_HALO, :] = upx_ref[tm:tm + FFN_HALO, :]


def _ffn_weight_specs():
  return [_full((GROUP_WIDTH, D_MODEL)), _full((D_MODEL, D_MODEL)), _full((1, D_MODEL)),
          _full((D_MODEL, 2 * D_FF)), _full((FFN_CONV_WIDTH, 2 * D_FF)), _full((1, 2 * D_FF)),
          _full((D_FF, D_MODEL)), _full((1, D_MODEL))]


def _ffn_prompt(x2, ga2, gate2, o2, weights, n_seq, t_len):
  tm = ROW_TILE
  nt = t_len // tm
  m = n_seq * t_len
  row = lambda i: (i, 0)
  return pl.pallas_call(
      functools.partial(_ffn_body, tm=tm, nt=nt, cached=False),
      grid=(m // tm,),
      in_specs=[pl.BlockSpec((tm, D_MODEL), row), pl.BlockSpec((tm, D_MODEL), row),
                pl.BlockSpec((tm, D_MODEL), lambda i: (i, 1)), pl.BlockSpec((tm, GROUP_WIDTH), row)]
      + _ffn_weight_specs(),
      out_specs=[pl.BlockSpec((tm, D_MODEL), row),
                 pl.BlockSpec((None, FFN_CONV_WIDTH - 1, 2 * D_FF), lambda i: (i // nt, 0, 0))],
      out_shape=[jax.ShapeDtypeStruct((m, D_MODEL), F32),
                 jax.ShapeDtypeStruct((n_seq, FFN_CONV_WIDTH - 1, 2 * D_FF), F32)],
      scratch_shapes=[pltpu.VMEM((FFN_HALO + tm, 2 * D_FF), F32)],
      compiler_params=_params(1), name="ffn_prompt",
  )(x2, ga2, gate2, o2, *weights)


def _ffn_sample(x2, ga2, gate2, o2, weights, hist):
  n = x2.shape[0]
  out_shape = [jax.ShapeDtypeStruct((n, D_MODEL), F32),
               jax.ShapeDtypeStruct((n, FFN_CONV_WIDTH - 1, 2 * D_FF), F32)]
  return pl.pallas_call(
      functools.partial(_ffn_body, tm=n, nt=1, cached=True),
      grid=(1,),
      in_specs=[_full((n, D_MODEL)), _full((n, D_MODEL)), pl.BlockSpec((n, D_MODEL), lambda i: (0, 1)),
                _full((n, GROUP_WIDTH))] + _ffn_weight_specs()
      + [_full((n, FFN_CONV_WIDTH - 1, 2 * D_FF))],
      out_specs=[_full(s.shape) for s in out_shape], out_shape=out_shape,
      compiler_params=_params(1), name="ffn_sample",
  )(x2, ga2, gate2, o2, *weights, hist)


def kernel(x_prompt, x_sample, state_conv, cache_k_w128, cache_v_w128, cache_k_w512, cache_v_w512,
           cache_k_w2048, cache_v_w2048, state_ffn_conv, g_mix, w_in, w_dw, b_dw, ln_g, ln_b,
           w_conv_out, w_attn_out, w_out, g_ffn, w_up, w_fdw, b_fdw, w_down, g_final):
  n_p, t_p, _ = x_prompt.shape
  n_s, t_s, _ = x_sample.shape
  assert g_mix.shape[0] == 1, "one layer"
  assert t_s == 1 and n_s % SAMPLE_BLOCK == 0
  assert t_p % ATTN_TILE == 0 and ATTN_TILE % ROW_TILE == 0
  caches = ((cache_k_w128[0], cache_v_w128[0]), (cache_k_w512[0], cache_v_w512[0]),
            (cache_k_w2048[0], cache_v_w2048[0]))
  for (window, _), (kc, vc) in zip(GROUPS, caches):
    assert kc.shape == (n_s, window, HEADS_PER_GROUP, HEAD_DIM) and vc.shape == kc.shape

  row = lambda a: a.reshape(1, -1)
  w_in_b = w_in[0].astype(BF16)
  w_co_b = w_conv_out[0].astype(BF16)
  ffn_weights = (w_attn_out[0].astype(BF16), w_out[0].astype(BF16), row(g_ffn[0]),
                 w_up[0].astype(BF16), w_fdw[0], row(b_fdw[0]), w_down[0].astype(BF16), row(g_final))
  conv_weights = (w_dw[0], row(b_dw[0]), row(ln_g[0]), row(ln_b[0]), w_co_b)

  xp2 = x_prompt.reshape(n_p * t_p, D_MODEL)
  outs = _inproj_prompt(xp2, row(g_mix[0]), w_in_b, _rope_tables(jnp.arange(t_p, dtype=jnp.int32)),
                        n_p, t_p)
  u_p, gate_p, qkvs, tails = outs[0], outs[1], outs[2:2 + N_GROUPS], outs[2 + N_GROUPS:]
  ga_p, conv_p = _conv_prompt(u_p, gate_p, *conv_weights, n_p, t_p)
  o_p = _attn_prompt(qkvs, n_p, t_p)
  y_p, ffn_p = _ffn_prompt(xp2, ga_p, gate_p, o_p, ffn_weights, n_p, t_p)

  xs2 = x_sample.reshape(n_s, D_MODEL)
  pos_s = jnp.full((n_s,), PAST_LEN, jnp.int32)
  u_s, gate_s, qkv_blocks = _inproj_sample(xs2, row(g_mix[0]), w_in_b, _rope_tables(pos_s))
  qkv_s = jnp.transpose(qkv_blocks, (1, 0, 2)).reshape(n_s, 3 * ATTN_WIDTH)
  ga_s, conv_s = _conv_sample(u_s, gate_s, state_conv[0], *conv_weights)
  heads = lambda a: a.reshape(n_s, N_GROUPS, HEADS_PER_GROUP, HEAD_DIM)
  q4 = heads(qkv_s[:, :ATTN_WIDTH])
  kn4 = heads(qkv_s[:, ATTN_WIDTH:2 * ATTN_WIDTH])
  vn4 = heads(qkv_s[:, 2 * ATTN_WIDTH:])
  o_s = _attn_sample(q4, kn4, vn4, caches)
  olds, news = [], []
  for gi, (kc, vc) in enumerate(caches):
    olds += [kc, vc]
    news += [kn4[:, gi][:, None], vn4[:, gi][:, None]]
  rolled = _cache_roll(olds, news)
  y_s, ffn_s = _ffn_sample(xs2, ga_s, gate_s, o_s.reshape(n_s, GROUP_WIDTH).astype(BF16), ffn_weights,
                           state_ffn_conv[0])

  kv_shape = lambda a: a.reshape(1, a.shape[0], a.shape[1], HEADS_PER_GROUP, HEAD_DIM)
  kv_p = [kv_shape(t) for t in tails]
  kv_s = [r[None] for r in rolled]
  return (y_p.reshape(n_p, t_p, D_MODEL), y_s.reshape(n_s, t_s, D_MODEL),
          conv_p[None], conv_s[None],
          kv_p[0], kv_p[1], kv_s[0], kv_s[1],
          kv_p[2], kv_p[3], kv_s[2], kv_s[3],
          kv_p[4], kv_p[5], kv_s[4], kv_s[5],
          ffn_p[None], ffn_s[None])
```

```python
import functools

import jax
import jax.numpy as jnp
from jax import lax
from jax.experimental import pallas as pl
from jax.experimental.pallas import tpu as pltpu

F32 = jnp.float32
BF16 = jnp.bfloat16

D_MODEL = 1024
HEAD_DIM = 64
HEADS_PER_GROUP = 4
GROUP_WIDTH = HEADS_PER_GROUP * HEAD_DIM
GROUPS = ((128, 1), (512, 4), (2048, 16))
N_GROUPS = len(GROUPS)
N_KEYS = 128
ATTN_WIDTH = N_GROUPS * GROUP_WIDTH
ROT_DIM = HEAD_DIM // 4
ROT_HALF = ROT_DIM // 2
ROPE_THETA = 500000.0
C_CONV = D_MODEL
CONV_WIDTH = 31
CONV_HALO = 32
D_FF = 2816
FFN_CONV_WIDTH = 3
FFN_HALO = 8
EPS = 1e-6
NEG = -1e30
PAST_LEN = 16384
IN_WIDTH = 2 * C_CONV + 3 * ATTN_WIDTH + 2 * D_MODEL
COL_Q = 2 * C_CONV
COL_K = COL_Q + ATTN_WIDTH
COL_V = COL_K + ATTN_WIDTH
COL_G = COL_V + ATTN_WIDTH

LANES = 128
QKV_BLOCKS = ATTN_WIDTH // LANES
ROW_TILE = 512
ATTN_TILE = 2048
FF_CHUNK = 256
CONV_ROWS = 64
MERGE_ROWS = 256
ATTN_UNROLL = 4
SOFTMAX_ROWS = 32
VMEM_LIMIT = 56 * 1024 * 1024


def _params(n_axes):
  return pltpu.CompilerParams(
      dimension_semantics=("arbitrary",) * n_axes, vmem_limit_bytes=VMEM_LIMIT)


def _full(shape):
  nd = len(shape)
  return pl.BlockSpec(shape, lambda *_: (0,) * nd)


def _sigmoid(x):
  return 1.0 / (1.0 + jnp.exp(-x))


def _rms_norm(x, g):
  return x * lax.rsqrt(jnp.mean(x * x, axis=-1, keepdims=True) + EPS) * g


def _rope_tables(pos):
  inv = ROPE_THETA ** (-jnp.arange(ROT_HALF, dtype=F32) / ROT_HALF)
  ang = pos.astype(F32)[:, None] * inv[None, :]
  cos, sin = jnp.cos(ang), jnp.sin(ang)
  n = pos.shape[0]
  one = jnp.ones((n, HEAD_DIM - ROT_DIM), F32)
  zero = jnp.zeros((n, HEAD_DIM - ROT_DIM), F32)
  z8 = jnp.zeros((n, ROT_HALF), F32)
  reps = LANES // HEAD_DIM
  c = jnp.tile(jnp.concatenate([cos, cos, one], axis=1), (1, reps))
  sa = jnp.tile(jnp.concatenate([-sin, z8, zero], axis=1), (1, reps))
  sb = jnp.tile(jnp.concatenate([z8, sin, zero], axis=1), (1, reps))
  return c, sa, sb


def _inproj_body(x_ref, g_ref, w_ref, cos_ref, sa_ref, sb_ref, u_ref, gate_ref, *rest, tm, dilated):
  h = _rms_norm(x_ref[...], g_ref[...]).astype(BF16)
  a = jnp.dot(h, w_ref[:, 0:2 * C_CONV], preferred_element_type=F32)
  u_ref[...] = a[:, :C_CONV] * _sigmoid(a[:, C_CONV:])
  gate_ref[...] = jnp.dot(h, w_ref[:, COL_G:IN_WIDTH], preferred_element_type=F32).astype(BF16)

  cos, sa, sb = cos_ref[...], sa_ref[...], sb_ref[...]

  def rope(z):
    return z * cos + pltpu.roll(z, LANES - ROT_HALF, 1) * sa + pltpu.roll(z, ROT_HALF, 1) * sb

  if dilated:
    qkv_refs, tail_refs, zs_ref = rest[:N_GROUPS], rest[N_GROUPS:3 * N_GROUPS], rest[3 * N_GROUPS]
  else:
    (zs_ref,) = rest
  zq = jnp.dot(h, w_ref[:, COL_Q:COL_K], preferred_element_type=F32)
  zk = jnp.dot(h, w_ref[:, COL_K:COL_V], preferred_element_type=F32)
  zv = jnp.dot(h, w_ref[:, COL_V:COL_G], preferred_element_type=F32)
  for s in range(QKV_BLOCKS):
    cols = slice(s * LANES, (s + 1) * LANES)
    zs_ref[s] = rope(zq[:, cols]) * (HEAD_DIM ** -0.5)
    zs_ref[QKV_BLOCKS + s] = rope(zk[:, cols])
    zs_ref[2 * QKV_BLOCKS + s] = zv[:, cols]
  if not dilated:
    return

  group_blocks = GROUP_WIDTH // LANES
  for gi, (window, dil) in enumerate(GROUPS):
    rows = tm // dil
    tw = min(tm, window)
    for part in range(3):
      for half in range(group_blocks):
        blk = part * QKV_BLOCKS + gi * group_blocks + half
        dst = slice(part * GROUP_WIDTH + half * LANES, part * GROUP_WIDTH + (half + 1) * LANES)
        for r in range(dil):
          qkv_refs[gi][r, :, dst] = zs_ref[blk, pl.ds(r, rows, stride=dil), :].astype(BF16)
        if part > 0:
          tail_refs[2 * gi + part - 1][:, half * LANES:(half + 1) * LANES] = zs_ref[blk, tm - tw:tm, :]


def _inproj_prompt(x2, g_mix, w_in, tables, n_seq, t_len):
  tm, ta = ROW_TILE, ATTN_TILE
  nt = t_len // tm
  m = n_seq * t_len
  row = lambda i: (i, 0)
  tab = lambda i: (i % nt, 0)
  in_specs = [pl.BlockSpec((tm, D_MODEL), row), _full((1, D_MODEL)), _full((D_MODEL, IN_WIDTH)),
              pl.BlockSpec((tm, LANES), tab), pl.BlockSpec((tm, LANES), tab), pl.BlockSpec((tm, LANES), tab)]
  out_shape = [jax.ShapeDtypeStruct((m, C_CONV), F32), jax.ShapeDtypeStruct((m, 2 * D_MODEL), BF16)]
  out_specs = [pl.BlockSpec((tm, C_CONV), row), pl.BlockSpec((tm, 2 * D_MODEL), row)]
  sub = ta // tm
  for window, dil in GROUPS:
    out_shape.append(jax.ShapeDtypeStruct((n_seq, t_len // ta, dil, ta // dil, 3 * GROUP_WIDTH), BF16))
    out_specs.append(pl.BlockSpec(
        (None, None, dil, tm // dil, 3 * GROUP_WIDTH),
        lambda i: (i // nt, (i % nt) // sub, 0, (i % nt) % sub, 0)))
  for window, dil in GROUPS:
    tw = min(tm, window)
    first = (t_len - window) // tm
    for _ in range(2):
      out_shape.append(jax.ShapeDtypeStruct((n_seq, window, GROUP_WIDTH), F32))
      out_specs.append(pl.BlockSpec(
          (None, tw, GROUP_WIDTH),
          functools.partial(lambda i, first: (i // nt, jnp.maximum(i % nt - first, 0), 0), first=first)))
  return pl.pallas_call(
      functools.partial(_inproj_body, tm=tm, dilated=True),
      grid=(m // tm,), in_specs=in_specs, out_specs=out_specs, out_shape=out_shape,
      scratch_shapes=[pltpu.VMEM((3 * QKV_BLOCKS, tm, LANES), F32)],
      compiler_params=_params(1), name="inproj_prompt",
  )(x2, g_mix, w_in, *tables)


def _inproj_sample(x2, g_mix, w_in, tables):
  m = x2.shape[0]
  out_shape = [jax.ShapeDtypeStruct((m, C_CONV), F32), jax.ShapeDtypeStruct((m, 2 * D_MODEL), BF16),
               jax.ShapeDtypeStruct((3 * QKV_BLOCKS, m, LANES), F32)]
  return pl.pallas_call(
      functools.partial(_inproj_body, tm=m, dilated=False),
      grid=(1,),
      in_specs=[_full((m, D_MODEL)), _full((1, D_MODEL)), _full((D_MODEL, IN_WIDTH)),
                _full((m, LANES)), _full((m, LANES)), _full((m, LANES))],
      out_specs=[_full(s.shape) for s in out_shape], out_shape=out_shape,
      compiler_params=_params(1), name="inproj_sample",
  )(x2, g_mix, w_in, *tables)


def _conv_post(c, ga, lng_ref, lnb_ref, wco_ref):
  mu = jnp.mean(c, axis=-1, keepdims=True)
  cc = c - mu
  var = jnp.mean(cc * cc, axis=-1, keepdims=True)
  y = cc * lax.rsqrt(var + EPS) * lng_ref[...] + lnb_ref[...]
  s = y * _sigmoid(y)
  out_a = jnp.dot(s.astype(BF16), wco_ref[...], preferred_element_type=F32)
  return (_sigmoid(ga.astype(F32)) * out_a).astype(BF16)


def _conv_prompt_body(u_ref, ga_ref, wdw_ref, bdw_ref, lng_ref, lnb_ref, wco_ref, out_ref, state_ref,
                      xe_ref, c_ref, *, tm, nt):
  it = pl.program_id(0) % nt

  n_cb = C_CONV // LANES

  @pl.when(it == 0)
  def _():
    xe_ref[:, 0:CONV_HALO, :] = jnp.zeros((n_cb, CONV_HALO, LANES), F32)

  lead = CONV_HALO - (CONV_WIDTH - 1)
  for cb in range(n_cb):
    cols = slice(cb * LANES, (cb + 1) * LANES)
    xe_ref[cb, CONV_HALO:CONV_HALO + tm, :] = u_ref[:, cols]
    def rows_body(rb, carry, cb=cb, cols=cols):
      r0 = pl.multiple_of(rb * CONV_ROWS, CONV_ROWS)
      acc = jnp.broadcast_to(bdw_ref[:, cols], (CONV_ROWS, LANES))
      for j in range(CONV_WIDTH):
        acc = acc + wdw_ref[j:j + 1, cols] * xe_ref[cb, pl.ds(r0 + (lead + j), CONV_ROWS), :]
      c_ref[pl.ds(r0, CONV_ROWS), cols] = acc
      return carry

    lax.fori_loop(0, tm // CONV_ROWS, rows_body, 0)
    state_ref[:, cols] = xe_ref[cb, CONV_HALO + tm - (CONV_WIDTH - 1):CONV_HALO + tm, :]

  out_ref[...] = _conv_post(c_ref[...], ga_ref[...], lng_ref, lnb_ref, wco_ref)
  xe_ref[:, 0:CONV_HALO, :] = xe_ref[:, tm:tm + CONV_HALO, :]


def _conv_prompt(u2, gate2, w_dw, b_dw, ln_g, ln_b, w_co, n_seq, t_len):
  tm = ROW_TILE
  nt = t_len // tm
  m = n_seq * t_len
  row = lambda i: (i, 0)
  return pl.pallas_call(
      functools.partial(_conv_prompt_body, tm=tm, nt=nt),
      grid=(m // tm,),
      in_specs=[pl.BlockSpec((tm, C_CONV), row), pl.BlockSpec((tm, D_MODEL), row),
                _full((CONV_WIDTH, C_CONV)), _full((1, C_CONV)), _full((1, C_CONV)), _full((1, C_CONV)),
                _full((C_CONV, D_MODEL))],
      out_specs=[pl.BlockSpec((tm, D_MODEL), row),
                 pl.BlockSpec((None, CONV_WIDTH - 1, C_CONV), lambda i: (i // nt, 0, 0))],
      out_shape=[jax.ShapeDtypeStruct((m, D_MODEL), BF16),
                 jax.ShapeDtypeStruct((n_seq, CONV_WIDTH - 1, C_CONV), F32)],
      scratch_shapes=[pltpu.VMEM((C_CONV // LANES, CONV_HALO + tm, LANES), F32),
                      pltpu.VMEM((tm, C_CONV), F32)],
      compiler_params=_params(1), name="conv_prompt",
  )(u2, gate2, w_dw, b_dw, ln_g, ln_b, w_co)


def _conv_sample_body(u_ref, ga_ref, st_ref, wdw_ref, bdw_ref, lng_ref, lnb_ref, wco_ref, out_ref,
                      state_ref):
  n = u_ref.shape[0]
  u = u_ref[...]
  acc = jnp.broadcast_to(bdw_ref[...], (n, C_CONV)) + wdw_ref[CONV_WIDTH - 1:CONV_WIDTH, :] * u
  for j in range(CONV_WIDTH - 1):
    acc = acc + wdw_ref[j:j + 1, :] * st_ref[:, j, :]
  out_ref[...] = _conv_post(acc, ga_ref[...], lng_ref, lnb_ref, wco_ref)
  for j in range(CONV_WIDTH - 2):
    state_ref[:, j, :] = st_ref[:, j + 1, :]
  state_ref[:, CONV_WIDTH - 2, :] = u


def _conv_sample(u2, gate2, state, w_dw, b_dw, ln_g, ln_b, w_co):
  n = u2.shape[0]
  out_shape = [jax.ShapeDtypeStruct((n, D_MODEL), BF16),
               jax.ShapeDtypeStruct((n, CONV_WIDTH - 1, C_CONV), F32)]
  return pl.pallas_call(
      _conv_sample_body, grid=(1,),
      in_specs=[_full((n, C_CONV)), pl.BlockSpec((n, D_MODEL), lambda i: (0, 0)),
                _full((n, CONV_WIDTH - 1, C_CONV)),
                _full((CONV_WIDTH, C_CONV)), _full((1, C_CONV)), _full((1, C_CONV)), _full((1, C_CONV)),
                _full((C_CONV, D_MODEL))],
      out_specs=[_full(s.shape) for s in out_shape], out_shape=out_shape,
      compiler_params=_params(1), name="conv_sample",
  )(u2, gate2, state, w_dw, b_dw, ln_g, ln_b, w_co)


def _attn_prompt_body(qkv0_ref, qkv1_ref, qkv2_ref, o_ref, ext0_ref, ext1_ref, ext2_ref, ores_ref,
                      lres_ref, onat_ref, lnat_ref, bias_ref, qs_ref, s_ref, p_ref, *, ta):
  first_tile = pl.program_id(1) == 0
  qi = lax.broadcasted_iota(jnp.int32, (N_KEYS, 2 * N_KEYS), 0)
  kj = lax.broadcasted_iota(jnp.int32, (N_KEYS, 2 * N_KEYS), 1)
  band = (kj >= qi) & (kj <= qi + N_KEYS)
  bias_ref[0] = jnp.where(band, 0.0, NEG)
  bias_ref[1] = jnp.where(band & (kj >= N_KEYS), 0.0, NEG)
  lane_head = lax.broadcasted_iota(jnp.int32, (N_KEYS, GROUP_WIDTH), 1) // HEAD_DIM
  head_masks = [lane_head == h for h in range(HEADS_PER_GROUP)]
  nt_dims = (((1,), (1,)), ((), ()))

  for gi, (qkv_ref, ext_ref) in enumerate(((qkv0_ref, ext0_ref), (qkv1_ref, ext1_ref),
                                           (qkv2_ref, ext2_ref))):
    dil = GROUPS[gi][1]
    rows = ta // dil
    nsb = rows // N_KEYS
    @pl.when(first_tile)
    def _(ext_ref=ext_ref, dil=dil):
      ext_ref[:, 0:N_KEYS, :] = jnp.zeros((dil, N_KEYS, 2 * GROUP_WIDTH), BF16)

    for r in range(dil):
      ext_ref[r, N_KEYS:N_KEYS + rows, :] = qkv_ref[r, :, GROUP_WIDTH:3 * GROUP_WIDTH]

    def unit_pair(uu, carry, qkv_ref=qkv_ref, ext_ref=ext_ref, nsb=nsb):
      units = []
      for slot in range(ATTN_UNROLL):
        u = uu * ATTN_UNROLL + slot
        r, sb = u // nsb, u % nsb
        units.append((slot, r, sb, pl.multiple_of(sb * N_KEYS, N_KEYS), pl.multiple_of(u * N_KEYS, N_KEYS)))
      for slot, r, sb, q0, o0 in units:
        q = qkv_ref[r, pl.ds(q0, N_KEYS), 0:GROUP_WIDTH]
        for h in range(HEADS_PER_GROUP):
          qs_ref[slot, h * N_KEYS:(h + 1) * N_KEYS, :] = jnp.where(head_masks[h], q, jnp.zeros_like(q))
        k = ext_ref[r, pl.ds(q0, 2 * N_KEYS), 0:GROUP_WIDTH]
        s_ref[slot] = lax.dot_general(qs_ref[slot], k, nt_dims, preferred_element_type=F32)
      for slot, r, sb, q0, o0 in units:
        hide_prev = (first_tile & (sb == 0)).astype(jnp.int32)
        for c in range(HEADS_PER_GROUP * N_KEYS // SOFTMAX_ROWS):
          h, qr = divmod(c * SOFTMAX_ROWS, N_KEYS)
          rows_c = slice(c * SOFTMAX_ROWS, (c + 1) * SOFTMAX_ROWS)
          s = s_ref[slot, rows_c, :] + bias_ref[hide_prev, qr:qr + SOFTMAX_ROWS, :]
          mx = jnp.max(s, axis=-1, keepdims=True)
          p = jnp.exp(s - mx)
          den = jnp.sum(p, axis=-1, keepdims=True)
          p_ref[slot, rows_c, :] = (p * (1.0 / den)).astype(BF16)
          lres_ref[pl.ds(o0 + qr, SOFTMAX_ROWS), h * HEAD_DIM:(h + 1) * HEAD_DIM] = jnp.broadcast_to(
              mx + jnp.log(den), (SOFTMAX_ROWS, HEAD_DIM))
      for slot, r, sb, q0, o0 in units:
        v = ext_ref[r, pl.ds(q0, 2 * N_KEYS), GROUP_WIDTH:2 * GROUP_WIDTH]
        pv = jnp.dot(p_ref[slot], v, preferred_element_type=F32)
        o_all = jnp.zeros((N_KEYS, GROUP_WIDTH), F32)
        for h in range(HEADS_PER_GROUP):
          o_all = jnp.where(head_masks[h], pv[h * N_KEYS:(h + 1) * N_KEYS], o_all)
        ores_ref[pl.ds(o0, N_KEYS), :] = o_all
      return carry

    lax.fori_loop(0, ta // (N_KEYS * ATTN_UNROLL), unit_pair, 0)

    for r in range(dil):
      ext_ref[r, 0:N_KEYS, :] = ext_ref[r, rows:rows + N_KEYS, :]
      for half in range(GROUP_WIDTH // LANES):
        cols = slice(half * LANES, (half + 1) * LANES)
        dst = pl.ds(r, rows, stride=dil)
        onat_ref[gi, half, dst, :] = ores_ref[r * rows:(r + 1) * rows, cols]
        lnat_ref[gi, half, dst, :] = lres_ref[r * rows:(r + 1) * rows, cols]

  def merge(c, carry):
    rs = pl.ds(pl.multiple_of(c * MERGE_ROWS, MERGE_ROWS), MERGE_ROWS)
    for half in range(GROUP_WIDTH // LANES):
      l0, l1, l2 = lnat_ref[0, half, rs, :], lnat_ref[1, half, rs, :], lnat_ref[2, half, rs, :]
      lm = jnp.maximum(jnp.maximum(l0, l1), l2)
      w0, w1, w2 = jnp.exp(l0 - lm), jnp.exp(l1 - lm), jnp.exp(l2 - lm)
      o = (w0 * onat_ref[0, half, rs, :] + w1 * onat_ref[1, half, rs, :]
           + w2 * onat_ref[2, half, rs, :]) / (w0 + w1 + w2)
      o_ref[rs, half * LANES:(half + 1) * LANES] = o.astype(BF16)
    return carry

  lax.fori_loop(0, ta // MERGE_ROWS, merge, 0)


def _attn_prompt(qkvs, n_seq, t_len):
  ta = ATTN_TILE
  in_specs, scratch = [], []
  for window, dil in GROUPS:
    in_specs.append(pl.BlockSpec((None, None, dil, ta // dil, 3 * GROUP_WIDTH),
                                 lambda b, t: (b, t, 0, 0, 0)))
    scratch.append(pltpu.VMEM((dil, N_KEYS + ta // dil, 2 * GROUP_WIDTH), BF16))
  scratch += [pltpu.VMEM((ta, GROUP_WIDTH), F32), pltpu.VMEM((ta, GROUP_WIDTH), F32),
              pltpu.VMEM((N_GROUPS, GROUP_WIDTH // LANES, ta, LANES), F32),
              pltpu.VMEM((N_GROUPS, GROUP_WIDTH // LANES, ta, LANES), F32),
              pltpu.VMEM((2, N_KEYS, 2 * N_KEYS), F32),
              pltpu.VMEM((ATTN_UNROLL, HEADS_PER_GROUP * N_KEYS, GROUP_WIDTH), BF16),
              pltpu.VMEM((ATTN_UNROLL, HEADS_PER_GROUP * N_KEYS, 2 * N_KEYS), F32),
              pltpu.VMEM((ATTN_UNROLL, HEADS_PER_GROUP * N_KEYS, 2 * N_KEYS), BF16)]
  return pl.pallas_call(
      functools.partial(_attn_prompt_body, ta=ta),
      grid=(n_seq, t_len // ta), in_specs=in_specs,
      out_specs=pl.BlockSpec((ta, GROUP_WIDTH), lambda b, t, nta=t_len // ta: (b * nta + t, 0)),
      out_shape=jax.ShapeDtypeStruct((n_seq * t_len, GROUP_WIDTH), BF16),
      scratch_shapes=scratch, compiler_params=_params(2), name="attn_prompt",
  )(*qkvs)


def _sample_attn_body(q_ref, kn_ref, vn_ref, k0_ref, v0_ref, k1_ref, v1_ref, k2_ref, v2_ref,
                      o_ref, ko0_ref, vo0_ref, ko1_ref, vo1_ref, ko2_ref, vo2_ref):
  caches = ((k0_ref, v0_ref, ko0_ref, vo0_ref), (k1_ref, v1_ref, ko1_ref, vo1_ref),
            (k2_ref, v2_ref, ko2_ref, vo2_ref))
  outs, lses = [], []
  for gi, (window, dil) in enumerate(GROUPS):
    k_ref, v_ref, ko_ref, vo_ref = caches[gi]
    q, kn, vn = q_ref[0, gi], kn_ref[0, gi], vn_ref[0, gi]
    k, v = k_ref[0], v_ref[0]
    lane = lax.broadcasted_iota(jnp.int32, (1, window), 1)
    in_window = lane % dil == 0
    qk = k * q
    qk_new = kn * q
    o_heads, l_heads = [], []
    for h in range(HEADS_PER_GROUP):
      hs = slice(h * HEAD_DIM, (h + 1) * HEAD_DIM)
      s = jnp.where(in_window, jnp.sum(qk[hs], axis=0, keepdims=True), NEG)
      s_new = jnp.sum(qk_new[hs], axis=0, keepdims=True)
      mx = jnp.maximum(jnp.max(s, axis=-1, keepdims=True), s_new)
      p = jnp.exp(s - mx)
      p_new = jnp.exp(s_new - mx)
      den = jnp.sum(p, axis=-1, keepdims=True) + p_new
      acc = jnp.sum(v[hs] * p, axis=-1, keepdims=True) + p_new * vn[hs]
      o_heads.append(acc / den)
      l_heads.append(jnp.broadcast_to(mx + jnp.log(den), (HEAD_DIM, 1)))
    outs.append(jnp.concatenate(o_heads, axis=0))
    lses.append(jnp.concatenate(l_heads, axis=0))
    last = lane == window - 1
    ko_ref[0] = jnp.where(last, kn, pltpu.roll(k, window - 1, 1))
    vo_ref[0] = jnp.where(last, vn, pltpu.roll(v, window - 1, 1))
  lm = jnp.maximum(jnp.maximum(lses[0], lses[1]), lses[2])
  ws = [jnp.exp(l - lm) for l in lses]
  o_ref[0] = (ws[0] * outs[0] + ws[1] * outs[1] + ws[2] * outs[2]) / (ws[0] + ws[1] + ws[2])


def _sample_attn(q, kn, vn, caches):
  n = q.shape[0]
  col = pl.BlockSpec((1, N_GROUPS, GROUP_WIDTH, 1), lambda i: (i, 0, 0, 0))
  in_specs, out_specs, out_shape, args = [col, col, col], [], [], []
  out_specs.append(pl.BlockSpec((1, GROUP_WIDTH, 1), lambda i: (i, 0, 0)))
  out_shape.append(jax.ShapeDtypeStruct((n, GROUP_WIDTH, 1), F32))
  for (window, dil), kv in zip(GROUPS, caches):
    for c in kv:
      args.append(c)
      in_specs.append(pl.BlockSpec((1, GROUP_WIDTH, window), lambda i: (i, 0, 0)))
      out_specs.append(pl.BlockSpec((1, GROUP_WIDTH, window), lambda i: (i, 0, 0)))
      out_shape.append(jax.ShapeDtypeStruct((n, GROUP_WIDTH, window), F32))
  return pl.pallas_call(
      _sample_attn_body, grid=(n,), in_specs=in_specs, out_specs=out_specs, out_shape=out_shape,
      compiler_params=_params(1), name="sample_attn",
  )(q, kn, vn, *args)


def _merge_branches(x_ref, ga_ref, gb_ref, o_ref, wao_ref, wo_ref):
  out_b = jnp.dot(o_ref[...], wao_ref[...], preferred_element_type=F32)
  mix = ga_ref[...].astype(F32) + _sigmoid(gb_ref[...].astype(F32)) * out_b
  return x_ref[...] + jnp.dot(mix.astype(BF16), wo_ref[...], preferred_element_type=F32)


def _gated(gate, val):
  return (gate * _sigmoid(gate) * val).astype(BF16)


def _ffn_prompt_body(x_ref, ga_ref, gb_ref, o_ref, wao_ref, wo_ref, gffn_ref, wup_ref, wfdw_ref, bfdw_ref,
                     wdn_ref, gfin_ref, y_ref, state_ref, h2_ref, acc_ref, upa_ref, upb_ref, carry_ref,
                     *, tm, nt):
  n_chunks = D_FF // FF_CHUNK
  x1 = _merge_branches(x_ref, ga_ref, gb_ref, o_ref, wao_ref, wo_ref)
  h2_ref[...] = _rms_norm(x1, gffn_ref[...]).astype(BF16)
  acc_ref[...] = x1

  @pl.when(pl.program_id(0) % nt == 0)
  def _():
    carry_ref[...] = jnp.zeros(carry_ref.shape, F32)

  def up_proj(c, up_ref):
    for half in range(2):
      up_ref[half, 0:FFN_HALO, :] = carry_ref[c, half]
      up_ref[half, FFN_HALO:FFN_HALO + tm, :] = jnp.dot(h2_ref[...], wup_ref[c, half],
                                                        preferred_element_type=F32)

  def finish(c, up_ref):
    halves = []
    for half in range(2):
      taps = wfdw_ref[c, half]
      halves.append(taps[0:1] * up_ref[half, FFN_HALO - 2:FFN_HALO - 2 + tm, :]
                    + taps[1:2] * up_ref[half, FFN_HALO - 1:FFN_HALO - 1 + tm, :]
                    + taps[2:3] * up_ref[half, FFN_HALO:FFN_HALO + tm, :] + bfdw_ref[c, half])
      carry_ref[c, half] = up_ref[half, tm:tm + FFN_HALO, :]
      state_ref[c, half] = up_ref[half, FFN_HALO + tm - (FFN_CONV_WIDTH - 1):FFN_HALO + tm, :]
    acc_ref[...] += jnp.dot(_gated(*halves), wdn_ref[c], preferred_element_type=F32)

  up_proj(0, upa_ref)

  def chunk_pair(i, carry):
    c = 2 * i
    up_proj(c + 1, upb_ref)
    finish(c, upa_ref)
    up_proj(c + 2, upa_ref)
    finish(c + 1, upb_ref)
    return carry

  assert n_chunks % 2 == 1
  lax.fori_loop(0, n_chunks // 2, chunk_pair, 0)
  finish(n_chunks - 1, upa_ref)
  y_ref[...] = _rms_norm(acc_ref[...], gfin_ref[...])


def _ffn_sample_body(x_ref, ga_ref, gb_ref, o_ref, wao_ref, wo_ref, gffn_ref, wup_ref, wfdw_ref, bfdw_ref,
                     wdn_ref, gfin_ref, hist_ref, y_ref, state_ref):
  x1 = _merge_branches(x_ref, ga_ref, gb_ref, o_ref, wao_ref, wo_ref)
  h2 = _rms_norm(x1, gffn_ref[...]).astype(BF16)
  acc = x1
  for c in range(D_FF // FF_CHUNK):
    halves = []
    for half in range(2):
      cols = slice(half * D_FF + c * FF_CHUNK, half * D_FF + (c + 1) * FF_CHUNK)
      up = jnp.dot(h2, wup_ref[c, half], preferred_element_type=F32)
      prev2, prev1 = hist_ref[:, 0, cols], hist_ref[:, 1, cols]
      state_ref[:, 0, cols] = prev1
      state_ref[:, 1, cols] = up
      taps = wfdw_ref[c, half]
      halves.append(taps[0:1] * prev2 + taps[1:2] * prev1 + taps[2:3] * up + bfdw_ref[c, half])
    acc = acc + jnp.dot(_gated(*halves), wdn_ref[c], preferred_element_type=F32)
  y_ref[...] = _rms_norm(acc, gfin_ref[...])


def _ffn_weight_specs():
  n_chunks = D_FF // FF_CHUNK
  return [_full((GROUP_WIDTH, D_MODEL)), _full((D_MODEL, D_MODEL)), _full((1, D_MODEL)),
          _full((n_chunks, 2, D_MODEL, FF_CHUNK)), _full((n_chunks, 2, FFN_CONV_WIDTH, FF_CHUNK)),
          _full((n_chunks, 2, 1, FF_CHUNK)), _full((n_chunks, FF_CHUNK, D_MODEL)), _full((1, D_MODEL))]


def _chunk_major(a):
  lead = a.shape[:-1]
  a = a.reshape(lead + (2, D_FF // FF_CHUNK, FF_CHUNK))
  nl = len(lead)
  return jnp.transpose(a, (nl + 1, nl) + tuple(range(nl)) + (nl + 2,))


def _ffn_prompt(x2, ga2, gate2, o2, weights, n_seq, t_len):
  tm = ROW_TILE
  nt = t_len // tm
  m = n_seq * t_len
  row = lambda i: (i, 0)
  n_chunks = D_FF // FF_CHUNK
  state_shape = (n_chunks, 2, FFN_CONV_WIDTH - 1, FF_CHUNK)
  up_slot = pltpu.VMEM((2, FFN_HALO + tm, FF_CHUNK), F32)
  y, state = pl.pallas_call(
      functools.partial(_ffn_prompt_body, tm=tm, nt=nt),
      grid=(m // tm,),
      in_specs=[pl.BlockSpec((tm, D_MODEL), row), pl.BlockSpec((tm, D_MODEL), row),
                pl.BlockSpec((tm, D_MODEL), lambda i: (i, 1)), pl.BlockSpec((tm, GROUP_WIDTH), row)]
      + _ffn_weight_specs(),
      out_specs=[pl.BlockSpec((tm, D_MODEL), row),
                 pl.BlockSpec((None,) + state_shape, lambda i: (i // nt, 0, 0, 0, 0))],
      out_shape=[jax.ShapeDtypeStruct((m, D_MODEL), F32),
                 jax.ShapeDtypeStruct((n_seq,) + state_shape, F32)],
      scratch_shapes=[pltpu.VMEM((tm, D_MODEL), BF16), pltpu.VMEM((tm, D_MODEL), F32), up_slot, up_slot,
                      pltpu.VMEM((n_chunks, 2, FFN_HALO, FF_CHUNK), F32)],
      compiler_params=_params(1), name="ffn_prompt",
  )(x2, ga2, gate2, o2, *weights)
  return y, jnp.transpose(state, (0, 3, 2, 1, 4)).reshape(n_seq, FFN_CONV_WIDTH - 1, 2 * D_FF)


def _ffn_sample(x2, ga2, gate2, o2, weights, hist):
  n = x2.shape[0]
  out_shape = [jax.ShapeDtypeStruct((n, D_MODEL), F32),
               jax.ShapeDtypeStruct((n, FFN_CONV_WIDTH - 1, 2 * D_FF), F32)]
  return pl.pallas_call(
      _ffn_sample_body,
      grid=(1,),
      in_specs=[_full((n, D_MODEL)), _full((n, D_MODEL)), pl.BlockSpec((n, D_MODEL), lambda i: (0, 1)),
                _full((n, GROUP_WIDTH))] + _ffn_weight_specs()
      + [_full((n, FFN_CONV_WIDTH - 1, 2 * D_FF))],
      out_specs=[_full(s.shape) for s in out_shape], out_shape=out_shape,
      compiler_params=_params(1), name="ffn_sample",
  )(x2, ga2, gate2, o2, *weights, hist)


def kernel(x_prompt, x_sample, state_conv, cache_k_w128, cache_v_w128, cache_k_w512, cache_v_w512,
           cache_k_w2048, cache_v_w2048, state_ffn_conv, g_mix, w_in, w_dw, b_dw, ln_g, ln_b,
           w_conv_out, w_attn_out, w_out, g_ffn, w_up, w_fdw, b_fdw, w_down, g_final):
  n_p, t_p, _ = x_prompt.shape
  n_s, t_s, _ = x_sample.shape
  assert g_mix.shape[0] == 1, "one layer"
  assert t_s == 1
  assert t_p % ATTN_TILE == 0 and ATTN_TILE % ROW_TILE == 0
  caches = ((cache_k_w128[0], cache_v_w128[0]), (cache_k_w512[0], cache_v_w512[0]),
            (cache_k_w2048[0], cache_v_w2048[0]))
  for (window, _), (kc, vc) in zip(GROUPS, caches):
    assert kc.shape == (n_s, window, HEADS_PER_GROUP, HEAD_DIM) and vc.shape == kc.shape

  row = lambda a: a.reshape(1, -1)
  w_in_b = w_in[0].astype(BF16)
  w_co_b = w_conv_out[0].astype(BF16)
  ffn_weights = (w_attn_out[0].astype(BF16), w_out[0].astype(BF16), row(g_ffn[0]),
                 _chunk_major(w_up[0].astype(BF16)), _chunk_major(w_fdw[0]), _chunk_major(row(b_fdw[0])),
                 w_down[0].astype(BF16).reshape(D_FF // FF_CHUNK, FF_CHUNK, D_MODEL), row(g_final))
  conv_weights = (w_dw[0], row(b_dw[0]), row(ln_g[0]), row(ln_b[0]), w_co_b)

  xp2 = x_prompt.reshape(n_p * t_p, D_MODEL)
  outs = _inproj_prompt(xp2, row(g_mix[0]), w_in_b, _rope_tables(jnp.arange(t_p, dtype=jnp.int32)),
                        n_p, t_p)
  u_p, gate_p, qkvs, tails = outs[0], outs[1], outs[2:2 + N_GROUPS], outs[2 + N_GROUPS:]
  ga_p, conv_p = _conv_prompt(u_p, gate_p, *conv_weights, n_p, t_p)
  o_p = _attn_prompt(qkvs, n_p, t_p)
  y_p, ffn_p = _ffn_prompt(xp2, ga_p, gate_p, o_p, ffn_weights, n_p, t_p)

  xs2 = x_sample.reshape(n_s, D_MODEL)
  pos_s = jnp.full((n_s,), PAST_LEN, jnp.int32)
  u_s, gate_s, qkv_blocks = _inproj_sample(xs2, row(g_mix[0]), w_in_b, _rope_tables(pos_s))
  qkv_s = jnp.transpose(qkv_blocks, (1, 0, 2)).reshape(n_s, 3 * ATTN_WIDTH)
  ga_s, conv_s = _conv_sample(u_s, gate_s, state_conv[0], *conv_weights)
  cols = lambda a: a.reshape(n_s, N_GROUPS, GROUP_WIDTH, 1)
  lanes_t = lambda c: jnp.transpose(c, (0, 2, 3, 1)).reshape(n_s, GROUP_WIDTH, c.shape[1])
  rows_t = lambda c: jnp.transpose(c.reshape(n_s, HEADS_PER_GROUP, HEAD_DIM, c.shape[2]), (0, 3, 1, 2))
  res = _sample_attn(cols(qkv_s[:, :ATTN_WIDTH]), cols(qkv_s[:, ATTN_WIDTH:2 * ATTN_WIDTH]),
                     cols(qkv_s[:, 2 * ATTN_WIDTH:]),
                     [(lanes_t(kc), lanes_t(vc)) for kc, vc in caches])
  o_s, rolled = res[0], res[1:]
  y_s, ffn_s = _ffn_sample(xs2, ga_s, gate_s, o_s.reshape(n_s, GROUP_WIDTH).astype(BF16), ffn_weights,
                           state_ffn_conv[0])

  kv_shape = lambda a: a.reshape(1, a.shape[0], a.shape[1], HEADS_PER_GROUP, HEAD_DIM)
  kv_p = [kv_shape(t) for t in tails]
  kv_s = [rows_t(r)[None] for r in rolled]
  return (y_p.reshape(n_p, t_p, D_MODEL), y_s.reshape(n_s, t_s, D_MODEL),
          conv_p[None], conv_s[None],
          kv_p[0], kv_p[1], kv_s[0], kv_s[1],
          kv_p[2], kv_p[3], kv_s[2], kv_s[3],
          kv_p[4], kv_p[5], kv_s[4], kv_s[5],
          ffn_p[None], ffn_s[None])
```

```python
import functools

import jax
import jax.numpy as jnp
from jax import lax
from jax.experimental import pallas as pl
from jax.experimental.pallas import tpu as pltpu

F32 = jnp.float32
BF16 = jnp.bfloat16

D_MODEL = 1024
HEAD_DIM = 64
HEADS_PER_GROUP = 4
GROUP_WIDTH = HEADS_PER_GROUP * HEAD_DIM
GROUPS = ((128, 1), (512, 4), (2048, 16))
N_GROUPS = len(GROUPS)
N_KEYS = 128
ATTN_WIDTH = N_GROUPS * GROUP_WIDTH
ROT_DIM = HEAD_DIM // 4
ROT_HALF = ROT_DIM // 2
ROPE_THETA = 500000.0
C_CONV = D_MODEL
CONV_WIDTH = 31
CONV_HALO = 32
D_FF = 2816
FFN_CONV_WIDTH = 3
FFN_HALO = 8
EPS = 1e-6
NEG = -1e30
PAST_LEN = 16384
IN_WIDTH = 2 * C_CONV + 3 * ATTN_WIDTH + 2 * D_MODEL
COL_Q = 2 * C_CONV
COL_K = COL_Q + ATTN_WIDTH
COL_V = COL_K + ATTN_WIDTH
COL_G = COL_V + ATTN_WIDTH

LANES = 128
QKV_BLOCKS = ATTN_WIDTH // LANES
ROW_TILE = 512
ATTN_TILE = 2048
FF_CHUNK = 256
CONV_ROWS = 32
MERGE_ROWS = 256
ATTN_UNROLL = 4
SOFTMAX_ROWS = 32
VMEM_LIMIT = 56 * 1024 * 1024


def _params(n_axes, flags=None):
  return pltpu.CompilerParams(
      dimension_semantics=("arbitrary",) * n_axes, vmem_limit_bytes=VMEM_LIMIT, flags=flags)


def _full(shape):
  nd = len(shape)
  return pl.BlockSpec(shape, lambda *_: (0,) * nd)


def _sigmoid(x):
  return 1.0 / (1.0 + jnp.exp(-x))


def _rms_norm(x, g):
  return x * lax.rsqrt(jnp.mean(x * x, axis=-1, keepdims=True) + EPS) * g


def _rope_tables(pos):
  inv = ROPE_THETA ** (-jnp.arange(ROT_HALF, dtype=F32) / ROT_HALF)
  ang = pos.astype(F32)[:, None] * inv[None, :]
  cos, sin = jnp.cos(ang), jnp.sin(ang)
  n = pos.shape[0]
  one = jnp.ones((n, HEAD_DIM - ROT_DIM), F32)
  zero = jnp.zeros((n, HEAD_DIM - ROT_DIM), F32)
  z8 = jnp.zeros((n, ROT_HALF), F32)
  reps = LANES // HEAD_DIM
  c = jnp.tile(jnp.concatenate([cos, cos, one], axis=1), (1, reps))
  sa = jnp.tile(jnp.concatenate([-sin, z8, zero], axis=1), (1, reps))
  sb = jnp.tile(jnp.concatenate([z8, sin, zero], axis=1), (1, reps))
  return c, sa, sb


def _glu(h, w_ref):
  a = jnp.dot(h, w_ref[:, 0:2 * C_CONV], preferred_element_type=F32)
  return a[:, :C_CONV] * _sigmoid(a[:, C_CONV:])


def _project_qkv(h, w_ref, cos_ref, sa_ref, sb_ref, zs_ref):
  cos, sa, sb = cos_ref[...], sa_ref[...], sb_ref[...]

  def rope(z):
    return z * cos + pltpu.roll(z, LANES - ROT_HALF, 1) * sa + pltpu.roll(z, ROT_HALF, 1) * sb

  zq = jnp.dot(h, w_ref[:, COL_Q:COL_K], preferred_element_type=F32)
  zk = jnp.dot(h, w_ref[:, COL_K:COL_V], preferred_element_type=F32)
  zv = jnp.dot(h, w_ref[:, COL_V:COL_G], preferred_element_type=F32)
  for s in range(QKV_BLOCKS):
    cols = slice(s * LANES, (s + 1) * LANES)
    zs_ref[s] = rope(zq[:, cols]) * (HEAD_DIM ** -0.5)
    zs_ref[QKV_BLOCKS + s] = rope(zk[:, cols])
    zs_ref[2 * QKV_BLOCKS + s] = zv[:, cols]


def _inproj_sample_body(x_ref, g_ref, w_ref, cos_ref, sa_ref, sb_ref, u_ref, gate_ref, zs_ref):
  h = _rms_norm(x_ref[...], g_ref[...]).astype(BF16)
  u_ref[...] = _glu(h, w_ref)
  gate_ref[...] = jnp.dot(h, w_ref[:, COL_G:IN_WIDTH], preferred_element_type=F32).astype(BF16)
  _project_qkv(h, w_ref, cos_ref, sa_ref, sb_ref, zs_ref)


def _inproj_prompt_body(x_ref, g_ref, w_ref, cos_ref, sa_ref, sb_ref, wdw_ref, bdw_ref, lng_ref, lnb_ref,
                        wco_ref, ga_ref, gb_ref, *rest, tm, nt):
  qkv_refs, tail_refs = rest[:N_GROUPS], rest[N_GROUPS:3 * N_GROUPS]
  state_ref, zs_ref, xe_ref, c_ref = rest[3 * N_GROUPS:]
  n_cb = C_CONV // LANES

  @pl.when(pl.program_id(0) % nt == 0)
  def _():
    xe_ref[:, 0:CONV_HALO, :] = jnp.zeros((n_cb, CONV_HALO, LANES), F32)

  h = _rms_norm(x_ref[...], g_ref[...]).astype(BF16)
  u = _glu(h, w_ref)
  lead = CONV_HALO - (CONV_WIDTH - 1)
  for cb in range(n_cb):
    cols = slice(cb * LANES, (cb + 1) * LANES)
    xe_ref[cb, CONV_HALO:CONV_HALO + tm, :] = u[:, cols]
    for r0 in range(0, tm, CONV_ROWS):
      acc = jnp.broadcast_to(bdw_ref[:, cols], (CONV_ROWS, LANES))
      for j in range(CONV_WIDTH):
        acc = acc + wdw_ref[j:j + 1, cols] * xe_ref[cb, r0 + lead + j:r0 + lead + j + CONV_ROWS, :]
      c_ref[r0:r0 + CONV_ROWS, cols] = acc
    state_ref[:, cols] = xe_ref[cb, CONV_HALO + tm - (CONV_WIDTH - 1):CONV_HALO + tm, :]
  xe_ref[:, 0:CONV_HALO, :] = xe_ref[:, tm:tm + CONV_HALO, :]

  gates = jnp.dot(h, w_ref[:, COL_G:IN_WIDTH], preferred_element_type=F32).astype(BF16)
  gb_ref[...] = gates[:, D_MODEL:]
  _project_qkv(h, w_ref, cos_ref, sa_ref, sb_ref, zs_ref)
  ga_ref[...] = _conv_post(c_ref[...], gates[:, :D_MODEL], lng_ref, lnb_ref, wco_ref)

  group_blocks = GROUP_WIDTH // LANES
  for gi, (window, dil) in enumerate(GROUPS):
    rows = tm // dil
    tw = min(tm, window)
    for part in range(3):
      for half in range(group_blocks):
        blk = part * QKV_BLOCKS + gi * group_blocks + half
        dst = slice(part * GROUP_WIDTH + half * LANES, part * GROUP_WIDTH + (half + 1) * LANES)
        for r in range(dil):
          qkv_refs[gi][r, :, dst] = zs_ref[blk, pl.ds(r, rows, stride=dil), :].astype(BF16)
        if part > 0:
          tail_refs[2 * gi + part - 1][:, half * LANES:(half + 1) * LANES] = zs_ref[blk, tm - tw:tm, :]


def _inproj_prompt(x2, g_mix, w_in, tables, conv_weights, n_seq, t_len):
  tm, ta = ROW_TILE, ATTN_TILE
  nt = t_len // tm
  m = n_seq * t_len
  row = lambda i: (i, 0)
  tab = lambda i: (i % nt, 0)
  in_specs = [pl.BlockSpec((tm, D_MODEL), row), _full((1, D_MODEL)), _full((D_MODEL, IN_WIDTH)),
              pl.BlockSpec((tm, LANES), tab), pl.BlockSpec((tm, LANES), tab), pl.BlockSpec((tm, LANES), tab),
              _full((CONV_WIDTH, C_CONV)), _full((1, C_CONV)), _full((1, C_CONV)), _full((1, C_CONV)),
              _full((C_CONV, D_MODEL))]
  out_shape = [jax.ShapeDtypeStruct((m, D_MODEL), BF16), jax.ShapeDtypeStruct((m, D_MODEL), BF16)]
  out_specs = [pl.BlockSpec((tm, D_MODEL), row), pl.BlockSpec((tm, D_MODEL), row)]
  sub = ta // tm
  for window, dil in GROUPS:
    out_shape.append(jax.ShapeDtypeStruct((n_seq, t_len // ta, dil, ta // dil, 3 * GROUP_WIDTH), BF16))
    out_specs.append(pl.BlockSpec(
        (None, None, dil, tm // dil, 3 * GROUP_WIDTH),
        lambda i: (i // nt, (i % nt) // sub, 0, (i % nt) % sub, 0)))
  for window, dil in GROUPS:
    tw = min(tm, window)
    first = (t_len - window) // tm
    for _ in range(2):
      out_shape.append(jax.ShapeDtypeStruct((n_seq, window, GROUP_WIDTH), F32))
      out_specs.append(pl.BlockSpec(
          (None, tw, GROUP_WIDTH),
          functools.partial(lambda i, first: (i // nt, jnp.maximum(i % nt - first, 0), 0), first=first)))
  out_shape.append(jax.ShapeDtypeStruct((n_seq, CONV_WIDTH - 1, C_CONV), F32))
  out_specs.append(pl.BlockSpec((None, CONV_WIDTH - 1, C_CONV), lambda i: (i // nt, 0, 0)))
  return pl.pallas_call(
      functools.partial(_inproj_prompt_body, tm=tm, nt=nt),
      grid=(m // tm,), in_specs=in_specs, out_specs=out_specs, out_shape=out_shape,
      scratch_shapes=[pltpu.VMEM((3 * QKV_BLOCKS, tm, LANES), F32),
                      pltpu.VMEM((C_CONV // LANES, CONV_HALO + tm, LANES), F32),
                      pltpu.VMEM((tm, C_CONV), F32)],
      compiler_params=_params(1), name="inproj_prompt",
  )(x2, g_mix, w_in, *tables, *conv_weights)


def _inproj_sample(x2, g_mix, w_in, tables):
  m = x2.shape[0]
  out_shape = [jax.ShapeDtypeStruct((m, C_CONV), F32), jax.ShapeDtypeStruct((m, 2 * D_MODEL), BF16),
               jax.ShapeDtypeStruct((3 * QKV_BLOCKS, m, LANES), F32)]
  return pl.pallas_call(
      _inproj_sample_body,
      grid=(1,),
      in_specs=[_full((m, D_MODEL)), _full((1, D_MODEL)), _full((D_MODEL, IN_WIDTH)),
                _full((m, LANES)), _full((m, LANES)), _full((m, LANES))],
      out_specs=[_full(s.shape) for s in out_shape], out_shape=out_shape,
      compiler_params=_params(1), name="inproj_sample",
  )(x2, g_mix, w_in, *tables)


def _conv_post(c, ga, lng_ref, lnb_ref, wco_ref):
  mu = jnp.mean(c, axis=-1, keepdims=True)
  cc = c - mu
  var = jnp.mean(cc * cc, axis=-1, keepdims=True)
  y = cc * lax.rsqrt(var + EPS) * lng_ref[...] + lnb_ref[...]
  s = y * _sigmoid(y)
  out_a = jnp.dot(s.astype(BF16), wco_ref[...], preferred_element_type=F32)
  return (_sigmoid(ga.astype(F32)) * out_a).astype(BF16)


def _conv_sample_body(u_ref, ga_ref, st_ref, wdw_ref, bdw_ref, lng_ref, lnb_ref, wco_ref, out_ref,
                      state_ref):
  n = u_ref.shape[0]
  u = u_ref[...]
  acc = jnp.broadcast_to(bdw_ref[...], (n, C_CONV)) + wdw_ref[CONV_WIDTH - 1:CONV_WIDTH, :] * u
  for j in range(CONV_WIDTH - 1):
    acc = acc + wdw_ref[j:j + 1, :] * st_ref[:, j, :]
  out_ref[...] = _conv_post(acc, ga_ref[...], lng_ref, lnb_ref, wco_ref)
  for j in range(CONV_WIDTH - 2):
    state_ref[:, j, :] = st_ref[:, j + 1, :]
  state_ref[:, CONV_WIDTH - 2, :] = u


def _conv_sample(u2, gate2, state, w_dw, b_dw, ln_g, ln_b, w_co):
  n = u2.shape[0]
  out_shape = [jax.ShapeDtypeStruct((n, D_MODEL), BF16),
               jax.ShapeDtypeStruct((n, CONV_WIDTH - 1, C_CONV), F32)]
  return pl.pallas_call(
      _conv_sample_body, grid=(1,),
      in_specs=[_full((n, C_CONV)), pl.BlockSpec((n, D_MODEL), lambda i: (0, 0)),
                _full((n, CONV_WIDTH - 1, C_CONV)),
                _full((CONV_WIDTH, C_CONV)), _full((1, C_CONV)), _full((1, C_CONV)), _full((1, C_CONV)),
                _full((C_CONV, D_MODEL))],
      out_specs=[_full(s.shape) for s in out_shape], out_shape=out_shape,
      compiler_params=_params(1), name="conv_sample",
  )(u2, gate2, state, w_dw, b_dw, ln_g, ln_b, w_co)


def _attn_prompt_body(qkv0_ref, qkv1_ref, qkv2_ref, o_ref, ext0_ref, ext1_ref, ext2_ref, ores_ref,
                      lres_ref, onat_ref, lnat_ref, bias_ref, qs_ref, s_ref, p_ref, *, ta):
  first_tile = pl.program_id(1) == 0
  qi = lax.broadcasted_iota(jnp.int32, (N_KEYS, 2 * N_KEYS), 0)
  kj = lax.broadcasted_iota(jnp.int32, (N_KEYS, 2 * N_KEYS), 1)
  band = (kj >= qi) & (kj <= qi + N_KEYS)
  bias_ref[0] = jnp.where(band, 0.0, NEG)
  bias_ref[1] = jnp.where(band & (kj >= N_KEYS), 0.0, NEG)
  lane_head = lax.broadcasted_iota(jnp.int32, (N_KEYS, GROUP_WIDTH), 1) // HEAD_DIM
  head_masks = [lane_head == h for h in range(HEADS_PER_GROUP)]
  nt_dims = (((1,), (1,)), ((), ()))

  for gi, (qkv_ref, ext_ref) in enumerate(((qkv0_ref, ext0_ref), (qkv1_ref, ext1_ref),
                                           (qkv2_ref, ext2_ref))):
    dil = GROUPS[gi][1]
    rows = ta // dil
    nsb = rows // N_KEYS
    @pl.when(first_tile)
    def _(ext_ref=ext_ref, dil=dil):
      ext_ref[:, 0:N_KEYS, :] = jnp.zeros((dil, N_KEYS, 2 * GROUP_WIDTH), BF16)

    for r in range(dil):
      ext_ref[r, N_KEYS:N_KEYS + rows, :] = qkv_ref[r, :, GROUP_WIDTH:3 * GROUP_WIDTH]

    def unit_pair(uu, carry, qkv_ref=qkv_ref, ext_ref=ext_ref, nsb=nsb):
      units = []
      for slot in range(ATTN_UNROLL):
        u = uu * ATTN_UNROLL + slot
        r, sb = u // nsb, u % nsb
        units.append((slot, r, sb, pl.multiple_of(sb * N_KEYS, N_KEYS), pl.multiple_of(u * N_KEYS, N_KEYS)))
      for slot, r, sb, q0, o0 in units:
        q = qkv_ref[r, pl.ds(q0, N_KEYS), 0:GROUP_WIDTH]
        for h in range(HEADS_PER_GROUP):
          qs_ref[slot, h * N_KEYS:(h + 1) * N_KEYS, :] = jnp.where(head_masks[h], q, jnp.zeros_like(q))
        k = ext_ref[r, pl.ds(q0, 2 * N_KEYS), 0:GROUP_WIDTH]
        s_ref[slot] = lax.dot_general(qs_ref[slot], k, nt_dims, preferred_element_type=F32)
      for slot, r, sb, q0, o0 in units:
        hide_prev = (first_tile & (sb == 0)).astype(jnp.int32)
        for c in range(HEADS_PER_GROUP * N_KEYS // SOFTMAX_ROWS):
          h, qr = divmod(c * SOFTMAX_ROWS, N_KEYS)
          rows_c = slice(c * SOFTMAX_ROWS, (c + 1) * SOFTMAX_ROWS)
          s = s_ref[slot, rows_c, :] + bias_ref[hide_prev, qr:qr + SOFTMAX_ROWS, :]
          mx = jnp.max(s, axis=-1, keepdims=True)
          p = jnp.exp(s - mx)
          den = jnp.sum(p, axis=-1, keepdims=True)
          p_ref[slot, rows_c, :] = (p * (1.0 / den)).astype(BF16)
          lres_ref[pl.ds(o0 + qr, SOFTMAX_ROWS), h * HEAD_DIM:(h + 1) * HEAD_DIM] = jnp.broadcast_to(
              mx + jnp.log(den), (SOFTMAX_ROWS, HEAD_DIM))
      for slot, r, sb, q0, o0 in units:
        v = ext_ref[r, pl.ds(q0, 2 * N_KEYS), GROUP_WIDTH:2 * GROUP_WIDTH]
        pv = jnp.dot(p_ref[slot], v, preferred_element_type=F32)
        o_all = jnp.zeros((N_KEYS, GROUP_WIDTH), F32)
        for h in range(HEADS_PER_GROUP):
          o_all = jnp.where(head_masks[h], pv[h * N_KEYS:(h + 1) * N_KEYS], o_all)
        ores_ref[pl.ds(o0, N_KEYS), :] = o_all
      return carry

    lax.fori_loop(0, ta // (N_KEYS * ATTN_UNROLL), unit_pair, 0)

    for r in range(dil):
      ext_ref[r, 0:N_KEYS, :] = ext_ref[r, rows:rows + N_KEYS, :]
      for half in range(GROUP_WIDTH // LANES):
        cols = slice(half * LANES, (half + 1) * LANES)
        dst = pl.ds(r, rows, stride=dil)
        onat_ref[gi, half, dst, :] = ores_ref[r * rows:(r + 1) * rows, cols]
        lnat_ref[gi, half, dst, :] = lres_ref[r * rows:(r + 1) * rows, cols]

  def merge(c, carry):
    rs = pl.ds(pl.multiple_of(c * MERGE_ROWS, MERGE_ROWS), MERGE_ROWS)
    for half in range(GROUP_WIDTH // LANES):
      l0, l1, l2 = lnat_ref[0, half, rs, :], lnat_ref[1, half, rs, :], lnat_ref[2, half, rs, :]
      lm = jnp.maximum(jnp.maximum(l0, l1), l2)
      w0, w1, w2 = jnp.exp(l0 - lm), jnp.exp(l1 - lm), jnp.exp(l2 - lm)
      o = (w0 * onat_ref[0, half, rs, :] + w1 * onat_ref[1, half, rs, :]
           + w2 * onat_ref[2, half, rs, :]) / (w0 + w1 + w2)
      o_ref[rs, half * LANES:(half + 1) * LANES] = o.astype(BF16)
    return carry

  lax.fori_loop(0, ta // MERGE_ROWS, merge, 0)


def _attn_prompt(qkvs, n_seq, t_len):
  ta = ATTN_TILE
  in_specs, scratch = [], []
  for window, dil in GROUPS:
    in_specs.append(pl.BlockSpec((None, None, dil, ta // dil, 3 * GROUP_WIDTH),
                                 lambda b, t: (b, t, 0, 0, 0)))
    scratch.append(pltpu.VMEM((dil, N_KEYS + ta // dil, 2 * GROUP_WIDTH), BF16))
  scratch += [pltpu.VMEM((ta, GROUP_WIDTH), F32), pltpu.VMEM((ta, GROUP_WIDTH), F32),
              pltpu.VMEM((N_GROUPS, GROUP_WIDTH // LANES, ta, LANES), F32),
              pltpu.VMEM((N_GROUPS, GROUP_WIDTH // LANES, ta, LANES), F32),
              pltpu.VMEM((2, N_KEYS, 2 * N_KEYS), F32),
              pltpu.VMEM((ATTN_UNROLL, HEADS_PER_GROUP * N_KEYS, GROUP_WIDTH), BF16),
              pltpu.VMEM((ATTN_UNROLL, HEADS_PER_GROUP * N_KEYS, 2 * N_KEYS), F32),
              pltpu.VMEM((ATTN_UNROLL, HEADS_PER_GROUP * N_KEYS, 2 * N_KEYS), BF16)]
  return pl.pallas_call(
      functools.partial(_attn_prompt_body, ta=ta),
      grid=(n_seq, t_len // ta), in_specs=in_specs,
      out_specs=pl.BlockSpec((ta, GROUP_WIDTH), lambda b, t, nta=t_len // ta: (b * nta + t, 0)),
      out_shape=jax.ShapeDtypeStruct((n_seq * t_len, GROUP_WIDTH), BF16),
      scratch_shapes=scratch, compiler_params=_params(2), name="attn_prompt",
  )(*qkvs)


def _sample_attn_body(q_ref, kn_ref, vn_ref, k0_ref, v0_ref, k1_ref, v1_ref, k2_ref, v2_ref,
                      o_ref, ko0_ref, vo0_ref, ko1_ref, vo1_ref, ko2_ref, vo2_ref):
  caches = ((k0_ref, v0_ref, ko0_ref, vo0_ref), (k1_ref, v1_ref, ko1_ref, vo1_ref),
            (k2_ref, v2_ref, ko2_ref, vo2_ref))
  outs, lses = [], []
  for gi, (window, dil) in enumerate(GROUPS):
    k_ref, v_ref, ko_ref, vo_ref = caches[gi]
    q, kn, vn = q_ref[0, gi], kn_ref[0, gi], vn_ref[0, gi]
    k, v = k_ref[0], v_ref[0]
    lane = lax.broadcasted_iota(jnp.int32, (1, window), 1)
    in_window = lane % dil == 0
    qk = k * q
    qk_new = kn * q
    o_heads, l_heads = [], []
    for h in range(HEADS_PER_GROUP):
      hs = slice(h * HEAD_DIM, (h + 1) * HEAD_DIM)
      s = jnp.where(in_window, jnp.sum(qk[hs], axis=0, keepdims=True), NEG)
      s_new = jnp.sum(qk_new[hs], axis=0, keepdims=True)
      mx = jnp.maximum(jnp.max(s, axis=-1, keepdims=True), s_new)
      p = jnp.exp(s - mx)
      p_new = jnp.exp(s_new - mx)
      den = jnp.sum(p, axis=-1, keepdims=True) + p_new
      acc = jnp.sum(v[hs] * p, axis=-1, keepdims=True) + p_new * vn[hs]
      o_heads.append(acc / den)
      l_heads.append(jnp.broadcast_to(mx + jnp.log(den), (HEAD_DIM, 1)))
    outs.append(jnp.concatenate(o_heads, axis=0))
    lses.append(jnp.concatenate(l_heads, axis=0))
    last = lane == window - 1
    ko_ref[0] = jnp.where(last, kn, pltpu.roll(k, window - 1, 1))
    vo_ref[0] = jnp.where(last, vn, pltpu.roll(v, window - 1, 1))
  lm = jnp.maximum(jnp.maximum(lses[0], lses[1]), lses[2])
  ws = [jnp.exp(l - lm) for l in lses]
  o_ref[0] = (ws[0] * outs[0] + ws[1] * outs[1] + ws[2] * outs[2]) / (ws[0] + ws[1] + ws[2])


def _sample_attn(q, kn, vn, caches):
  n = q.shape[0]
  col = pl.BlockSpec((1, N_GROUPS, GROUP_WIDTH, 1), lambda i: (i, 0, 0, 0))
  in_specs, out_specs, out_shape, args = [col, col, col], [], [], []
  out_specs.append(pl.BlockSpec((1, GROUP_WIDTH, 1), lambda i: (i, 0, 0)))
  out_shape.append(jax.ShapeDtypeStruct((n, GROUP_WIDTH, 1), F32))
  for (window, dil), kv in zip(GROUPS, caches):
    for c in kv:
      args.append(c)
      in_specs.append(pl.BlockSpec((1, GROUP_WIDTH, window), lambda i: (i, 0, 0)))
      out_specs.append(pl.BlockSpec((1, GROUP_WIDTH, window), lambda i: (i, 0, 0)))
      out_shape.append(jax.ShapeDtypeStruct((n, GROUP_WIDTH, window), F32))
  return pl.pallas_call(
      _sample_attn_body, grid=(n,), in_specs=in_specs, out_specs=out_specs, out_shape=out_shape,
      compiler_params=_params(1), name="sample_attn",
  )(q, kn, vn, *args)


def _merge_branches(x_ref, ga_ref, gb_ref, o_ref, wao_ref, wo_ref):
  out_b = jnp.dot(o_ref[...], wao_ref[...], preferred_element_type=F32)
  mix = ga_ref[...].astype(F32) + _sigmoid(gb_ref[...].astype(F32)) * out_b
  return x_ref[...] + jnp.dot(mix.astype(BF16), wo_ref[...], preferred_element_type=F32)


def _gated(gate, val):
  return (gate * _sigmoid(gate) * val).astype(BF16)


def _ffn_prompt_body(x_ref, ga_ref, gb_ref, o_ref, wao_ref, wo_ref, gffn_ref, wup_ref, wfdw_ref, bfdw_ref,
                     wdn_ref, gfin_ref, y_ref, state_ref, h2_ref, acc_ref, upa_ref, upb_ref, carry_ref,
                     *, tm, nt):
  n_chunks = D_FF // FF_CHUNK
  x1 = _merge_branches(x_ref, ga_ref, gb_ref, o_ref, wao_ref, wo_ref)
  h2_ref[...] = _rms_norm(x1, gffn_ref[...]).astype(BF16)
  acc_ref[...] = x1

  @pl.when(pl.program_id(0) % nt == 0)
  def _():
    carry_ref[...] = jnp.zeros(carry_ref.shape, F32)

  lane_blocks = FF_CHUNK // LANES
  tail = slice(FFN_HALO + tm - (FFN_CONV_WIDTH - 1), FFN_HALO + tm)

  def up_proj(c, up_ref):
    for half in range(2):
      up = jnp.dot(h2_ref[...], wup_ref[c, half], preferred_element_type=F32)
      for lb in range(lane_blocks):
        up_ref[half, lb, 0:FFN_HALO, :] = carry_ref[c, half, lb]
        up_ref[half, lb, FFN_HALO:FFN_HALO + tm, :] = up[:, lb * LANES:(lb + 1) * LANES]

  def finish(c, up_ref):
    acts = []
    for lb in range(lane_blocks):
      cols = slice(lb * LANES, (lb + 1) * LANES)
      halves = []
      for half in range(2):
        taps = wfdw_ref[c, half]
        halves.append(taps[0:1, cols] * up_ref[half, lb, FFN_HALO - 2:FFN_HALO - 2 + tm, :]
                      + taps[1:2, cols] * up_ref[half, lb, FFN_HALO - 1:FFN_HALO - 1 + tm, :]
                      + taps[2:3, cols] * up_ref[half, lb, FFN_HALO:FFN_HALO + tm, :]
                      + bfdw_ref[c, half][:, cols])
        carry_ref[c, half, lb] = up_ref[half, lb, tm:tm + FFN_HALO, :]
        state_ref[c, half, :, cols] = up_ref[half, lb, tail, :]
      acts.append(_gated(*halves))
    acc_ref[...] += jnp.dot(jnp.concatenate(acts, axis=1), wdn_ref[c], preferred_element_type=F32)

  up_proj(0, upa_ref)

  def chunk_pair(i, carry):
    c = 2 * i
    up_proj(c + 1, upb_ref)
    finish(c, upa_ref)
    up_proj(c + 2, upa_ref)
    finish(c + 1, upb_ref)
    return carry

  assert n_chunks % 2 == 1
  lax.fori_loop(0, n_chunks // 2, chunk_pair, 0)
  finish(n_chunks - 1, upa_ref)
  y_ref[...] = _rms_norm(acc_ref[...], gfin_ref[...])


def _ffn_sample_body(x_ref, ga_ref, gb_ref, o_ref, wao_ref, wo_ref, gffn_ref, wup_ref, wfdw_ref, bfdw_ref,
                     wdn_ref, gfin_ref, hist_ref, y_ref, state_ref):
  x1 = _merge_branches(x_ref, ga_ref, gb_ref, o_ref, wao_ref, wo_ref)
  h2 = _rms_norm(x1, gffn_ref[...]).astype(BF16)
  acc = x1
  for c in range(D_FF // FF_CHUNK):
    halves = []
    for half in range(2):
      cols = slice(half * D_FF + c * FF_CHUNK, half * D_FF + (c + 1) * FF_CHUNK)
      up = jnp.dot(h2, wup_ref[c, half], preferred_element_type=F32)
      prev2, prev1 = hist_ref[:, 0, cols], hist_ref[:, 1, cols]
      state_ref[:, 0, cols] = prev1
      state_ref[:, 1, cols] = up
      taps = wfdw_ref[c, half]
      halves.append(taps[0:1] * prev2 + taps[1:2] * prev1 + taps[2:3] * up + bfdw_ref[c, half])
    acc = acc + jnp.dot(_gated(*halves), wdn_ref[c], preferred_element_type=F32)
  y_ref[...] = _rms_norm(acc, gfin_ref[...])


def _ffn_weight_specs():
  n_chunks = D_FF // FF_CHUNK
  return [_full((GROUP_WIDTH, D_MODEL)), _full((D_MODEL, D_MODEL)), _full((1, D_MODEL)),
          _full((n_chunks, 2, D_MODEL, FF_CHUNK)), _full((n_chunks, 2, FFN_CONV_WIDTH, FF_CHUNK)),
          _full((n_chunks, 2, 1, FF_CHUNK)), _full((n_chunks, FF_CHUNK, D_MODEL)), _full((1, D_MODEL))]


def _chunk_major(a):
  lead = a.shape[:-1]
  a = a.reshape(lead + (2, D_FF // FF_CHUNK, FF_CHUNK))
  nl = len(lead)
  return jnp.transpose(a, (nl + 1, nl) + tuple(range(nl)) + (nl + 2,))


def _ffn_prompt(x2, ga2, gate2, o2, weights, n_seq, t_len):
  tm = ROW_TILE
  nt = t_len // tm
  m = n_seq * t_len
  row = lambda i: (i, 0)
  n_chunks = D_FF // FF_CHUNK
  state_shape = (n_chunks, 2, FFN_CONV_WIDTH - 1, FF_CHUNK)
  up_slot = pltpu.VMEM((2, FF_CHUNK // LANES, FFN_HALO + tm, LANES), F32)
  y, state = pl.pallas_call(
      functools.partial(_ffn_prompt_body, tm=tm, nt=nt),
      grid=(m // tm,),
      in_specs=[pl.BlockSpec((tm, D_MODEL), row), pl.BlockSpec((tm, D_MODEL), row),
                pl.BlockSpec((tm, D_MODEL), row), pl.BlockSpec((tm, GROUP_WIDTH), row)]
      + _ffn_weight_specs(),
      out_specs=[pl.BlockSpec((tm, D_MODEL), row),
                 pl.BlockSpec((None,) + state_shape, lambda i: (i // nt, 0, 0, 0, 0))],
      out_shape=[jax.ShapeDtypeStruct((m, D_MODEL), F32),
                 jax.ShapeDtypeStruct((n_seq,) + state_shape, F32)],
      scratch_shapes=[pltpu.VMEM((tm, D_MODEL), BF16), pltpu.VMEM((tm, D_MODEL), F32), up_slot, up_slot,
                      pltpu.VMEM((n_chunks, 2, FF_CHUNK // LANES, FFN_HALO, LANES), F32)],
      compiler_params=_params(1), name="ffn_prompt",
  )(x2, ga2, gate2, o2, *weights)
  return y, jnp.transpose(state, (0, 3, 2, 1, 4)).reshape(n_seq, FFN_CONV_WIDTH - 1, 2 * D_FF)


def _ffn_sample(x2, ga2, gate2, o2, weights, hist):
  n = x2.shape[0]
  out_shape = [jax.ShapeDtypeStruct((n, D_MODEL), F32),
               jax.ShapeDtypeStruct((n, FFN_CONV_WIDTH - 1, 2 * D_FF), F32)]
  return pl.pallas_call(
      _ffn_sample_body,
      grid=(1,),
      in_specs=[_full((n, D_MODEL)), _full((n, D_MODEL)), pl.BlockSpec((n, D_MODEL), lambda i: (0, 1)),
                _full((n, GROUP_WIDTH))] + _ffn_weight_specs()
      + [_full((n, FFN_CONV_WIDTH - 1, 2 * D_FF))],
      out_specs=[_full(s.shape) for s in out_shape], out_shape=out_shape,
      compiler_params=_params(1), name="ffn_sample",
  )(x2, ga2, gate2, o2, *weights, hist)


def kernel(x_prompt, x_sample, state_conv, cache_k_w128, cache_v_w128, cache_k_w512, cache_v_w512,
           cache_k_w2048, cache_v_w2048, state_ffn_conv, g_mix, w_in, w_dw, b_dw, ln_g, ln_b,
           w_conv_out, w_attn_out, w_out, g_ffn, w_up, w_fdw, b_fdw, w_down, g_final):
  n_p, t_p, _ = x_prompt.shape
  n_s, t_s, _ = x_sample.shape
  assert g_mix.shape[0] == 1, "one layer"
  assert t_s == 1
  assert t_p % ATTN_TILE == 0 and ATTN_TILE % ROW_TILE == 0
  caches = ((cache_k_w128[0], cache_v_w128[0]), (cache_k_w512[0], cache_v_w512[0]),
            (cache_k_w2048[0], cache_v_w2048[0]))
  for (window, _), (kc, vc) in zip(GROUPS, caches):
    assert kc.shape == (n_s, window, HEADS_PER_GROUP, HEAD_DIM) and vc.shape == kc.shape

  row = lambda a: a.reshape(1, -1)
  w_in_b = w_in[0].astype(BF16)
  w_co_b = w_conv_out[0].astype(BF16)
  ffn_weights = (w_attn_out[0].astype(BF16), w_out[0].astype(BF16), row(g_ffn[0]),
                 _chunk_major(w_up[0].astype(BF16)), _chunk_major(w_fdw[0]), _chunk_major(row(b_fdw[0])),
                 w_down[0].astype(BF16).reshape(D_FF // FF_CHUNK, FF_CHUNK, D_MODEL), row(g_final))
  conv_weights = (w_dw[0], row(b_dw[0]), row(ln_g[0]), row(ln_b[0]), w_co_b)

  xp2 = x_prompt.reshape(n_p * t_p, D_MODEL)
  outs = _inproj_prompt(xp2, row(g_mix[0]), w_in_b, _rope_tables(jnp.arange(t_p, dtype=jnp.int32)),
                        conv_weights, n_p, t_p)
  ga_p, gb_p, qkvs = outs[0], outs[1], outs[2:2 + N_GROUPS]
  tails, conv_p = outs[2 + N_GROUPS:2 + 3 * N_GROUPS], outs[2 + 3 * N_GROUPS]
  o_p = _attn_prompt(qkvs, n_p, t_p)
  y_p, ffn_p = _ffn_prompt(xp2, ga_p, gb_p, o_p, ffn_weights, n_p, t_p)

  xs2 = x_sample.reshape(n_s, D_MODEL)
  pos_s = jnp.full((n_s,), PAST_LEN, jnp.int32)
  u_s, gate_s, qkv_blocks = _inproj_sample(xs2, row(g_mix[0]), w_in_b, _rope_tables(pos_s))
  qkv_s = jnp.transpose(qkv_blocks, (1, 0, 2)).reshape(n_s, 3 * ATTN_WIDTH)
  ga_s, conv_s = _conv_sample(u_s, gate_s, state_conv[0], *conv_weights)
  cols = lambda a: a.reshape(n_s, N_GROUPS, GROUP_WIDTH, 1)
  lanes_t = lambda c: jnp.transpose(c, (0, 2, 3, 1)).reshape(n_s, GROUP_WIDTH, c.shape[1])
  rows_t = lambda c: jnp.transpose(c.reshape(n_s, HEADS_PER_GROUP, HEAD_DIM, c.shape[2]), (0, 3, 1, 2))
  res = _sample_attn(cols(qkv_s[:, :ATTN_WIDTH]), cols(qkv_s[:, ATTN_WIDTH:2 * ATTN_WIDTH]),
                     cols(qkv_s[:, 2 * ATTN_WIDTH:]),
                     [(lanes_t(kc), lanes_t(vc)) for kc, vc in caches])
  o_s, rolled = res[0], res[1:]
  y_s, ffn_s = _ffn_sample(xs2, ga_s, gate_s, o_s.reshape(n_s, GROUP_WIDTH).astype(BF16), ffn_weights,
                           state_ffn_conv[0])

  kv_shape = lambda a: a.reshape(1, a.shape[0], a.shape[1], HEADS_PER_GROUP, HEAD_DIM)
  kv_p = [kv_shape(t) for t in tails]
  kv_s = [rows_t(r)[None] for r in rolled]
  return (y_p.reshape(n_p, t_p, D_MODEL), y_s.reshape(n_s, t_s, D_MODEL),
          conv_p[None], conv_s[None],
          kv_p[0], kv_p[1], kv_s[0], kv_s[1],
          kv_p[2], kv_p[3], kv_s[2], kv_s[3],
          kv_p[4], kv_p[5], kv_s[4], kv_s[5],
          ffn_p[None], ffn_s[None])
```

```python
import functools

import jax
import jax.numpy as jnp
from jax import lax
from jax.experimental import pallas as pl
from jax.experimental.pallas import tpu as pltpu

F32 = jnp.float32
BF16 = jnp.bfloat16

D_MODEL = 1024
HEAD_DIM = 64
HEADS_PER_GROUP = 4
GROUP_WIDTH = HEADS_PER_GROUP * HEAD_DIM
GROUPS = ((128, 1), (512, 4), (2048, 16))
N_GROUPS = len(GROUPS)
N_KEYS = 128
ATTN_WIDTH = N_GROUPS * GROUP_WIDTH
ROT_DIM = HEAD_DIM // 4
ROT_HALF = ROT_DIM // 2
ROPE_THETA = 500000.0
C_CONV = D_MODEL
CONV_WIDTH = 31
CONV_HALO = 32
D_FF = 2816
FFN_CONV_WIDTH = 3
FFN_HALO = 8
EPS = 1e-6
NEG = -1e30
PAST_LEN = 16384
IN_WIDTH = 2 * C_CONV + 3 * ATTN_WIDTH + 2 * D_MODEL
COL_Q = 2 * C_CONV
COL_K = COL_Q + ATTN_WIDTH
COL_V = COL_K + ATTN_WIDTH
COL_G = COL_V + ATTN_WIDTH

LANES = 128
QKV_BLOCKS = ATTN_WIDTH // LANES
ROW_TILE = 512
ATTN_TILE = 2048
FF_CHUNK = 256
CONV_ROWS = 64
SLICE_BLOCKS = 4
TAIL_BLOCKS = 2
MERGE_ROWS = 256
ATTN_UNROLL = 4
SOFTMAX_ROWS = 32
VMEM_LIMIT = 56 * 1024 * 1024


def _params(n_axes, flags=None):
  return pltpu.CompilerParams(
      dimension_semantics=("arbitrary",) * n_axes, vmem_limit_bytes=VMEM_LIMIT, flags=flags)


def _full(shape):
  nd = len(shape)
  return pl.BlockSpec(shape, lambda *_: (0,) * nd)


def _sigmoid(x):
  return 1.0 / (1.0 + jnp.exp(-x))


def _rms_norm(x, g):
  return x * lax.rsqrt(jnp.mean(x * x, axis=-1, keepdims=True) + EPS) * g


def _rope_tables(pos):
  inv = ROPE_THETA ** (-jnp.arange(ROT_HALF, dtype=F32) / ROT_HALF)
  ang = pos.astype(F32)[:, None] * inv[None, :]
  cos, sin = jnp.cos(ang), jnp.sin(ang)
  n = pos.shape[0]
  one = jnp.ones((n, HEAD_DIM - ROT_DIM), F32)
  zero = jnp.zeros((n, HEAD_DIM - ROT_DIM), F32)
  z8 = jnp.zeros((n, ROT_HALF), F32)
  reps = LANES // HEAD_DIM
  c = jnp.tile(jnp.concatenate([cos, cos, one], axis=1), (1, reps))
  sa = jnp.tile(jnp.concatenate([-sin, z8, zero], axis=1), (1, reps))
  sb = jnp.tile(jnp.concatenate([z8, sin, zero], axis=1), (1, reps))
  return c, sa, sb


def _glu(h, w_ref):
  a = jnp.dot(h, w_ref[:, 0:2 * C_CONV], preferred_element_type=F32)
  return a[:, :C_CONV] * _sigmoid(a[:, C_CONV:])


def _project_qkv(h, w_ref, cos_ref, sa_ref, sb_ref, zs_ref):
  cos, sa, sb = cos_ref[...], sa_ref[...], sb_ref[...]

  def rope(z):
    return z * cos + pltpu.roll(z, LANES - ROT_HALF, 1) * sa + pltpu.roll(z, ROT_HALF, 1) * sb

  zq = jnp.dot(h, w_ref[:, COL_Q:COL_K], preferred_element_type=F32)
  zk = jnp.dot(h, w_ref[:, COL_K:COL_V], preferred_element_type=F32)
  zv = jnp.dot(h, w_ref[:, COL_V:COL_G], preferred_element_type=F32)
  for s in range(QKV_BLOCKS):
    cols = slice(s * LANES, (s + 1) * LANES)
    zs_ref[s] = rope(zq[:, cols]) * (HEAD_DIM ** -0.5)
    zs_ref[QKV_BLOCKS + s] = rope(zk[:, cols])
    zs_ref[2 * QKV_BLOCKS + s] = zv[:, cols]


def _inproj_sample_body(x_ref, g_ref, w_ref, cos_ref, sa_ref, sb_ref, u_ref, gate_ref, zs_ref):
  h = _rms_norm(x_ref[...], g_ref[...]).astype(BF16)
  u_ref[...] = _glu(h, w_ref)
  gate_ref[...] = jnp.dot(h, w_ref[:, COL_G:IN_WIDTH], preferred_element_type=F32).astype(BF16)
  _project_qkv(h, w_ref, cos_ref, sa_ref, sb_ref, zs_ref)


def _inproj_prompt_body(x_ref, g_ref, wglu_ref, wloop_ref, wtail_ref, cos_ref, sa_ref, sb_ref, wdw_ref,
                        bdw_ref, lng_ref, lnb_ref, wco_ref, ga_ref, gb_ref, *rest, tm, nt):
  qkv_refs, tail_refs = rest[:N_GROUPS], rest[N_GROUPS:3 * N_GROUPS]
  state_ref, h_ref, z_ref, xe_ref, c_ref = rest[3 * N_GROUPS:]
  n_cb = C_CONV // LANES

  @pl.when(pl.program_id(0) % nt == 0)
  def _():
    xe_ref[:, 0:CONV_HALO, :] = jnp.zeros((n_cb, CONV_HALO, LANES), F32)

  h_ref[...] = _rms_norm(x_ref[...], g_ref[...]).astype(BF16)
  u = _glu(h_ref[...], wglu_ref)
  for cb in range(n_cb):
    xe_ref[cb, CONV_HALO:CONV_HALO + tm, :] = u[:, cb * LANES:(cb + 1) * LANES]
  lead = CONV_HALO - (CONV_WIDTH - 1)

  def slice_and_conv(cb, carry):
    zc = jnp.dot(h_ref[...], wloop_ref[cb], preferred_element_type=F32)
    for k in range(SLICE_BLOCKS):
      z_ref[cb * SLICE_BLOCKS + k] = zc[:, k * LANES:(k + 1) * LANES]
    for r0 in range(0, tm, CONV_ROWS):
      acc = jnp.broadcast_to(bdw_ref[cb], (CONV_ROWS, LANES))
      for j in range(CONV_WIDTH):
        acc = acc + wdw_ref[cb, j:j + 1, :] * xe_ref[cb, r0 + lead + j:r0 + lead + j + CONV_ROWS, :]
      c_ref[cb, r0:r0 + CONV_ROWS, :] = acc
    state_ref[cb] = xe_ref[cb, CONV_HALO + tm - (CONV_WIDTH - 1):CONV_HALO + tm, :]
    xe_ref[cb, 0:CONV_HALO, :] = xe_ref[cb, tm:tm + CONV_HALO, :]
    return carry

  lax.fori_loop(0, n_cb, slice_and_conv, 0)

  cos, sa, sb = cos_ref[...], sa_ref[...], sb_ref[...]

  def rope(z):
    return z * cos + pltpu.roll(z, LANES - ROT_HALF, 1) * sa + pltpu.roll(z, ROT_HALF, 1) * sb

  for s in range(QKV_BLOCKS):
    z_ref[s] = rope(z_ref[s]) * (HEAD_DIM ** -0.5)
    z_ref[QKV_BLOCKS + s] = rope(z_ref[QKV_BLOCKS + s])

  gate_a0 = 3 * QKV_BLOCKS
  gate_blocks = D_MODEL // LANES
  for k in range(gate_blocks - TAIL_BLOCKS):
    gb_ref[:, k * LANES:(k + 1) * LANES] = z_ref[gate_a0 + gate_blocks + k].astype(BF16)
  gb_ref[:, (gate_blocks - TAIL_BLOCKS) * LANES:] = jnp.dot(
      h_ref[...], wtail_ref[...], preferred_element_type=F32).astype(BF16)
  c = jnp.concatenate([c_ref[cb] for cb in range(n_cb)], axis=1)
  g_a = jnp.concatenate([z_ref[gate_a0 + k] for k in range(gate_blocks)], axis=1)
  ga_ref[...] = _conv_post(c, g_a, lng_ref, lnb_ref, wco_ref)

  group_blocks = GROUP_WIDTH // LANES
  for gi, (window, dil) in enumerate(GROUPS):
    rows = tm // dil
    tw = min(tm, window)
    for part in range(3):
      for half in range(group_blocks):
        blk = part * QKV_BLOCKS + gi * group_blocks + half
        dst = slice(part * GROUP_WIDTH + half * LANES, part * GROUP_WIDTH + (half + 1) * LANES)
        for r in range(dil):
          qkv_refs[gi][r, :, dst] = z_ref[blk, pl.ds(r, rows, stride=dil), :].astype(BF16)
        if part > 0:
          tail_refs[2 * gi + part - 1][:, half * LANES:(half + 1) * LANES] = z_ref[blk, tm - tw:tm, :]


def _inproj_prompt(x2, g_mix, w_in, tables, conv_weights, n_seq, t_len):
  tm, ta = ROW_TILE, ATTN_TILE
  nt = t_len // tm
  m = n_seq * t_len
  n_cb = C_CONV // LANES
  loop_cols = n_cb * SLICE_BLOCKS * LANES
  assert COL_Q + loop_cols + TAIL_BLOCKS * LANES == IN_WIDTH
  w_glu = w_in[:, :COL_Q]
  w_loop = jnp.transpose(w_in[:, COL_Q:COL_Q + loop_cols].reshape(D_MODEL, n_cb, SLICE_BLOCKS * LANES),
                         (1, 0, 2))
  w_tail = w_in[:, COL_Q + loop_cols:]
  w_dw, b_dw, ln_g, ln_b, w_co = conv_weights
  w_dw_c = jnp.transpose(w_dw.reshape(CONV_WIDTH, n_cb, LANES), (1, 0, 2))
  b_dw_c = b_dw.reshape(n_cb, 1, LANES)
  row = lambda i: (i, 0)
  tab = lambda i: (i % nt, 0)
  in_specs = [pl.BlockSpec((tm, D_MODEL), row), _full((1, D_MODEL)),
              _full(w_glu.shape), _full(w_loop.shape), _full(w_tail.shape),
              pl.BlockSpec((tm, LANES), tab), pl.BlockSpec((tm, LANES), tab), pl.BlockSpec((tm, LANES), tab),
              _full(w_dw_c.shape), _full(b_dw_c.shape), _full((1, C_CONV)), _full((1, C_CONV)),
              _full((C_CONV, D_MODEL))]
  out_shape = [jax.ShapeDtypeStruct((m, D_MODEL), BF16), jax.ShapeDtypeStruct((m, D_MODEL), BF16)]
  out_specs = [pl.BlockSpec((tm, D_MODEL), row), pl.BlockSpec((tm, D_MODEL), row)]
  sub = ta // tm
  for window, dil in GROUPS:
    out_shape.append(jax.ShapeDtypeStruct((n_seq, t_len // ta, dil, ta // dil, 3 * GROUP_WIDTH), BF16))
    out_specs.append(pl.BlockSpec(
        (None, None, dil, tm // dil, 3 * GROUP_WIDTH),
        lambda i: (i // nt, (i % nt) // sub, 0, (i % nt) % sub, 0)))
  for window, dil in GROUPS:
    tw = min(tm, window)
    first = (t_len - window) // tm
    for _ in range(2):
      out_shape.append(jax.ShapeDtypeStruct((n_seq, window, GROUP_WIDTH), F32))
      out_specs.append(pl.BlockSpec(
          (None, tw, GROUP_WIDTH),
          functools.partial(lambda i, first: (i // nt, jnp.maximum(i % nt - first, 0), 0), first=first)))
  out_shape.append(jax.ShapeDtypeStruct((n_seq, n_cb, CONV_WIDTH - 1, LANES), F32))
  out_specs.append(pl.BlockSpec((None, n_cb, CONV_WIDTH - 1, LANES), lambda i: (i // nt, 0, 0, 0)))
  outs = pl.pallas_call(
      functools.partial(_inproj_prompt_body, tm=tm, nt=nt),
      grid=(m // tm,), in_specs=in_specs, out_specs=out_specs, out_shape=out_shape,
      scratch_shapes=[pltpu.VMEM((tm, D_MODEL), BF16),
                      pltpu.VMEM((n_cb * SLICE_BLOCKS, tm, LANES), F32),
                      pltpu.VMEM((n_cb, CONV_HALO + tm, LANES), F32),
                      pltpu.VMEM((n_cb, tm, LANES), F32)],
      compiler_params=_params(1), name="inproj_prompt",
  )(x2, g_mix, w_glu, w_loop, w_tail, *tables, w_dw_c, b_dw_c, ln_g, ln_b, w_co)
  state = jnp.transpose(outs[-1], (0, 2, 1, 3)).reshape(n_seq, CONV_WIDTH - 1, C_CONV)
  return list(outs[:-1]) + [state]


def _inproj_sample(x2, g_mix, w_in, tables):
  m = x2.shape[0]
  out_shape = [jax.ShapeDtypeStruct((m, C_CONV), F32), jax.ShapeDtypeStruct((m, 2 * D_MODEL), BF16),
               jax.ShapeDtypeStruct((3 * QKV_BLOCKS, m, LANES), F32)]
  return pl.pallas_call(
      _inproj_sample_body,
      grid=(1,),
      in_specs=[_full((m, D_MODEL)), _full((1, D_MODEL)), _full((D_MODEL, IN_WIDTH)),
                _full((m, LANES)), _full((m, LANES)), _full((m, LANES))],
      out_specs=[_full(s.shape) for s in out_shape], out_shape=out_shape,
      compiler_params=_params(1), name="inproj_sample",
  )(x2, g_mix, w_in, *tables)


def _conv_post(c, ga, lng_ref, lnb_ref, wco_ref):
  mu = jnp.mean(c, axis=-1, keepdims=True)
  cc = c - mu
  var = jnp.mean(cc * cc, axis=-1, keepdims=True)
  y = cc * lax.rsqrt(var + EPS) * lng_ref[...] + lnb_ref[...]
  s = y * _sigmoid(y)
  out_a = jnp.dot(s.astype(BF16), wco_ref[...], preferred_element_type=F32)
  return (_sigmoid(ga.astype(F32)) * out_a).astype(BF16)


def _conv_sample_body(u_ref, ga_ref, st_ref, wdw_ref, bdw_ref, lng_ref, lnb_ref, wco_ref, out_ref,
                      state_ref):
  n = u_ref.shape[0]
  u = u_ref[...]
  acc = jnp.broadcast_to(bdw_ref[...], (n, C_CONV)) + wdw_ref[CONV_WIDTH - 1:CONV_WIDTH, :] * u
  for j in range(CONV_WIDTH - 1):
    acc = acc + wdw_ref[j:j + 1, :] * st_ref[:, j, :]
  out_ref[...] = _conv_post(acc, ga_ref[...], lng_ref, lnb_ref, wco_ref)
  for j in range(CONV_WIDTH - 2):
    state_ref[:, j, :] = st_ref[:, j + 1, :]
  state_ref[:, CONV_WIDTH - 2, :] = u


def _conv_sample(u2, gate2, state, w_dw, b_dw, ln_g, ln_b, w_co):
  n = u2.shape[0]
  out_shape = [jax.ShapeDtypeStruct((n, D_MODEL), BF16),
               jax.ShapeDtypeStruct((n, CONV_WIDTH - 1, C_CONV), F32)]
  return pl.pallas_call(
      _conv_sample_body, grid=(1,),
      in_specs=[_full((n, C_CONV)), pl.BlockSpec((n, D_MODEL), lambda i: (0, 0)),
                _full((n, CONV_WIDTH - 1, C_CONV)),
                _full((CONV_WIDTH, C_CONV)), _full((1, C_CONV)), _full((1, C_CONV)), _full((1, C_CONV)),
                _full((C_CONV, D_MODEL))],
      out_specs=[_full(s.shape) for s in out_shape], out_shape=out_shape,
      compiler_params=_params(1), name="conv_sample",
  )(u2, gate2, state, w_dw, b_dw, ln_g, ln_b, w_co)


def _attn_prompt_body(qkv0_ref, qkv1_ref, qkv2_ref, o_ref, ext0_ref, ext1_ref, ext2_ref, ores_ref,
                      lres_ref, onat_ref, lnat_ref, bias_ref, qs_ref, s_ref, p_ref, *, ta):
  first_tile = pl.program_id(1) == 0
  qi = lax.broadcasted_iota(jnp.int32, (N_KEYS, 2 * N_KEYS), 0)
  kj = lax.broadcasted_iota(jnp.int32, (N_KEYS, 2 * N_KEYS), 1)
  band = (kj >= qi) & (kj <= qi + N_KEYS)
  bias_ref[0] = jnp.where(band, 0.0, NEG)
  bias_ref[1] = jnp.where(band & (kj >= N_KEYS), 0.0, NEG)
  lane_head = lax.broadcasted_iota(jnp.int32, (N_KEYS, GROUP_WIDTH), 1) // HEAD_DIM
  head_masks = [lane_head == h for h in range(HEADS_PER_GROUP)]
  nt_dims = (((1,), (1,)), ((), ()))

  for gi, (qkv_ref, ext_ref) in enumerate(((qkv0_ref, ext0_ref), (qkv1_ref, ext1_ref),
                                           (qkv2_ref, ext2_ref))):
    dil = GROUPS[gi][1]
    rows = ta // dil
    nsb = rows // N_KEYS
    @pl.when(first_tile)
    def _(ext_ref=ext_ref, dil=dil):
      ext_ref[:, 0:N_KEYS, :] = jnp.zeros((dil, N_KEYS, 2 * GROUP_WIDTH), BF16)

    for r in range(dil):
      ext_ref[r, N_KEYS:N_KEYS + rows, :] = qkv_ref[r, :, GROUP_WIDTH:3 * GROUP_WIDTH]

    def unit_pair(uu, carry, qkv_ref=qkv_ref, ext_ref=ext_ref, nsb=nsb):
      units = []
      for slot in range(ATTN_UNROLL):
        u = uu * ATTN_UNROLL + slot
        r, sb = u // nsb, u % nsb
        units.append((slot, r, sb, pl.multiple_of(sb * N_KEYS, N_KEYS), pl.multiple_of(u * N_KEYS, N_KEYS)))
      for slot, r, sb, q0, o0 in units:
        q = qkv_ref[r, pl.ds(q0, N_KEYS), 0:GROUP_WIDTH]
        for h in range(HEADS_PER_GROUP):
          qs_ref[slot, h * N_KEYS:(h + 1) * N_KEYS, :] = jnp.where(head_masks[h], q, jnp.zeros_like(q))
        k = ext_ref[r, pl.ds(q0, 2 * N_KEYS), 0:GROUP_WIDTH]
        s_ref[slot] = lax.dot_general(qs_ref[slot], k, nt_dims, preferred_element_type=F32)
      for slot, r, sb, q0, o0 in units:
        hide_prev = (first_tile & (sb == 0)).astype(jnp.int32)
        for c in range(HEADS_PER_GROUP * N_KEYS // SOFTMAX_ROWS):
          h, qr = divmod(c * SOFTMAX_ROWS, N_KEYS)
          rows_c = slice(c * SOFTMAX_ROWS, (c + 1) * SOFTMAX_ROWS)
          s = s_ref[slot, rows_c, :] + bias_ref[hide_prev, qr:qr + SOFTMAX_ROWS, :]
          mx = jnp.max(s, axis=-1, keepdims=True)
          p = jnp.exp(s - mx)
          den = jnp.sum(p, axis=-1, keepdims=True)
          p_ref[slot, rows_c, :] = (p * (1.0 / den)).astype(BF16)
          lres_ref[pl.ds(o0 + qr, SOFTMAX_ROWS), h * HEAD_DIM:(h + 1) * HEAD_DIM] = jnp.broadcast_to(
              mx + jnp.log(den), (SOFTMAX_ROWS, HEAD_DIM))
      for slot, r, sb, q0, o0 in units:
        v = ext_ref[r, pl.ds(q0, 2 * N_KEYS), GROUP_WIDTH:2 * GROUP_WIDTH]
        pv = jnp.dot(p_ref[slot], v, preferred_element_type=F32)
        for h in range(HEADS_PER_GROUP):
          lanes_h = slice(h * HEAD_DIM, (h + 1) * HEAD_DIM)
          ores_ref[pl.ds(o0, N_KEYS), lanes_h] = pv[h * N_KEYS:(h + 1) * N_KEYS, lanes_h]
      return carry

    lax.fori_loop(0, ta // (N_KEYS * ATTN_UNROLL), unit_pair, 0)

    for r in range(dil):
      ext_ref[r, 0:N_KEYS, :] = ext_ref[r, rows:rows + N_KEYS, :]
      for half in range(GROUP_WIDTH // LANES):
        cols = slice(half * LANES, (half + 1) * LANES)
        dst = pl.ds(r, rows, stride=dil)
        onat_ref[gi, half, dst, :] = ores_ref[r * rows:(r + 1) * rows, cols]
        lnat_ref[gi, half, dst, :] = lres_ref[r * rows:(r + 1) * rows, cols]

  def merge(c, carry):
    rs = pl.ds(pl.multiple_of(c * MERGE_ROWS, MERGE_ROWS), MERGE_ROWS)
    for half in range(GROUP_WIDTH // LANES):
      l0, l1, l2 = lnat_ref[0, half, rs, :], lnat_ref[1, half, rs, :], lnat_ref[2, half, rs, :]
      lm = jnp.maximum(jnp.maximum(l0, l1), l2)
      w0, w1, w2 = jnp.exp(l0 - lm), jnp.exp(l1 - lm), jnp.exp(l2 - lm)
      o = (w0 * onat_ref[0, half, rs, :] + w1 * onat_ref[1, half, rs, :]
           + w2 * onat_ref[2, half, rs, :]) / (w0 + w1 + w2)
      o_ref[rs, half * LANES:(half + 1) * LANES] = o.astype(BF16)
    return carry

  lax.fori_loop(0, ta // MERGE_ROWS, merge, 0)


def _attn_prompt(qkvs, n_seq, t_len):
  ta = ATTN_TILE
  in_specs, scratch = [], []
  for window, dil in GROUPS:
    in_specs.append(pl.BlockSpec((None, None, dil, ta // dil, 3 * GROUP_WIDTH),
                                 lambda b, t: (b, t, 0, 0, 0)))
    scratch.append(pltpu.VMEM((dil, N_KEYS + ta // dil, 2 * GROUP_WIDTH), BF16))
  scratch += [pltpu.VMEM((ta, GROUP_WIDTH), F32), pltpu.VMEM((ta, GROUP_WIDTH), F32),
              pltpu.VMEM((N_GROUPS, GROUP_WIDTH // LANES, ta, LANES), F32),
              pltpu.VMEM((N_GROUPS, GROUP_WIDTH // LANES, ta, LANES), F32),
              pltpu.VMEM((2, N_KEYS, 2 * N_KEYS), F32),
              pltpu.VMEM((ATTN_UNROLL, HEADS_PER_GROUP * N_KEYS, GROUP_WIDTH), BF16),
              pltpu.VMEM((ATTN_UNROLL, HEADS_PER_GROUP * N_KEYS, 2 * N_KEYS), F32),
              pltpu.VMEM((ATTN_UNROLL, HEADS_PER_GROUP * N_KEYS, 2 * N_KEYS), BF16)]
  return pl.pallas_call(
      functools.partial(_attn_prompt_body, ta=ta),
      grid=(n_seq, t_len // ta), in_specs=in_specs,
      out_specs=pl.BlockSpec((ta, GROUP_WIDTH), lambda b, t, nta=t_len // ta: (b * nta + t, 0)),
      out_shape=jax.ShapeDtypeStruct((n_seq * t_len, GROUP_WIDTH), BF16),
      scratch_shapes=scratch, compiler_params=_params(2), name="attn_prompt",
  )(*qkvs)


def _sample_attn_body(q_ref, kn_ref, vn_ref, k0_ref, v0_ref, k1_ref, v1_ref, k2_ref, v2_ref,
                      o_ref, ko0_ref, vo0_ref, ko1_ref, vo1_ref, ko2_ref, vo2_ref):
  caches = ((k0_ref, v0_ref, ko0_ref, vo0_ref), (k1_ref, v1_ref, ko1_ref, vo1_ref),
            (k2_ref, v2_ref, ko2_ref, vo2_ref))
  outs, lses = [], []
  for gi, (window, dil) in enumerate(GROUPS):
    k_ref, v_ref, ko_ref, vo_ref = caches[gi]
    q, kn, vn = q_ref[0, gi], kn_ref[0, gi], vn_ref[0, gi]
    k, v = k_ref[0], v_ref[0]
    lane = lax.broadcasted_iota(jnp.int32, (1, window), 1)
    in_window = lane % dil == 0
    qk = k * q
    qk_new = kn * q
    o_heads, l_heads = [], []
    for h in range(HEADS_PER_GROUP):
      hs = slice(h * HEAD_DIM, (h + 1) * HEAD_DIM)
      s = jnp.where(in_window, jnp.sum(qk[hs], axis=0, keepdims=True), NEG)
      s_new = jnp.sum(qk_new[hs], axis=0, keepdims=True)
      mx = jnp.maximum(jnp.max(s, axis=-1, keepdims=True), s_new)
      p = jnp.exp(s - mx)
      p_new = jnp.exp(s_new - mx)
      den = jnp.sum(p, axis=-1, keepdims=True) + p_new
      acc = jnp.sum(v[hs] * p, axis=-1, keepdims=True) + p_new * vn[hs]
      o_heads.append(acc / den)
      l_heads.append(jnp.broadcast_to(mx + jnp.log(den), (HEAD_DIM, 1)))
    outs.append(jnp.concatenate(o_heads, axis=0))
    lses.append(jnp.concatenate(l_heads, axis=0))
    last = lane == window - 1
    ko_ref[0] = jnp.where(last, kn, pltpu.roll(k, window - 1, 1))
    vo_ref[0] = jnp.where(last, vn, pltpu.roll(v, window - 1, 1))
  lm = jnp.maximum(jnp.maximum(lses[0], lses[1]), lses[2])
  ws = [jnp.exp(l - lm) for l in lses]
  o_ref[0] = (ws[0] * outs[0] + ws[1] * outs[1] + ws[2] * outs[2]) / (ws[0] + ws[1] + ws[2])


def _sample_attn(q, kn, vn, caches):
  n = q.shape[0]
  col = pl.BlockSpec((1, N_GROUPS, GROUP_WIDTH, 1), lambda i: (i, 0, 0, 0))
  in_specs, out_specs, out_shape, args = [col, col, col], [], [], []
  out_specs.append(pl.BlockSpec((1, GROUP_WIDTH, 1), lambda i: (i, 0, 0)))
  out_shape.append(jax.ShapeDtypeStruct((n, GROUP_WIDTH, 1), F32))
  for (window, dil), kv in zip(GROUPS, caches):
    for c in kv:
      args.append(c)
      in_specs.append(pl.BlockSpec((1, GROUP_WIDTH, window), lambda i: (i, 0, 0)))
      out_specs.append(pl.BlockSpec((1, GROUP_WIDTH, window), lambda i: (i, 0, 0)))
      out_shape.append(jax.ShapeDtypeStruct((n, GROUP_WIDTH, window), F32))
  return pl.pallas_call(
      _sample_attn_body, grid=(n,), in_specs=in_specs, out_specs=out_specs, out_shape=out_shape,
      compiler_params=_params(1), name="sample_attn",
  )(q, kn, vn, *args)


def _merge_branches(x_ref, ga_ref, gb_ref, o_ref, wao_ref, wo_ref):
  out_b = jnp.dot(o_ref[...], wao_ref[...], preferred_element_type=F32)
  mix = ga_ref[...].astype(F32) + _sigmoid(gb_ref[...].astype(F32)) * out_b
  return x_ref[...] + jnp.dot(mix.astype(BF16), wo_ref[...], preferred_element_type=F32)


def _gated(gate, val):
  return (gate * _sigmoid(gate) * val).astype(BF16)


def _ffn_prompt_body(x_ref, ga_ref, gb_ref, o_ref, wao_ref, wo_ref, gffn_ref, wup_ref, wfdw_ref, bfdw_ref,
                     wdn_ref, gfin_ref, y_ref, state_ref, h2_ref, acc_ref, upa_ref, upb_ref, carry_ref,
                     *, tm, nt):
  n_chunks = D_FF // FF_CHUNK
  x1 = _merge_branches(x_ref, ga_ref, gb_ref, o_ref, wao_ref, wo_ref)
  h2_ref[...] = _rms_norm(x1, gffn_ref[...]).astype(BF16)
  acc_ref[...] = x1

  @pl.when(pl.program_id(0) % nt == 0)
  def _():
    carry_ref[...] = jnp.zeros(carry_ref.shape, F32)

  lane_blocks = FF_CHUNK // LANES
  tail = slice(FFN_HALO + tm - (FFN_CONV_WIDTH - 1), FFN_HALO + tm)

  def up_proj(c, up_ref):
    for half in range(2):
      up = jnp.dot(h2_ref[...], wup_ref[c, half], preferred_element_type=F32)
      for lb in range(lane_blocks):
        up_ref[half, lb, 0:FFN_HALO, :] = carry_ref[c, half, lb]
        up_ref[half, lb, FFN_HALO:FFN_HALO + tm, :] = up[:, lb * LANES:(lb + 1) * LANES]

  def finish(c, up_ref):
    acts = []
    for lb in range(lane_blocks):
      cols = slice(lb * LANES, (lb + 1) * LANES)
      halves = []
      for half in range(2):
        taps = wfdw_ref[c, half]
        halves.append(taps[0:1, cols] * up_ref[half, lb, FFN_HALO - 2:FFN_HALO - 2 + tm, :]
                      + taps[1:2, cols] * up_ref[half, lb, FFN_HALO - 1:FFN_HALO - 1 + tm, :]
                      + taps[2:3, cols] * up_ref[half, lb, FFN_HALO:FFN_HALO + tm, :]
                      + bfdw_ref[c, half][:, cols])
        carry_ref[c, half, lb] = up_ref[half, lb, tm:tm + FFN_HALO, :]
        state_ref[c, half, :, cols] = up_ref[half, lb, tail, :]
      acts.append(_gated(*halves))
    acc_ref[...] += jnp.dot(jnp.concatenate(acts, axis=1), wdn_ref[c], preferred_element_type=F32)

  up_proj(0, upa_ref)

  def chunk_pair(i, carry):
    c = 2 * i
    up_proj(c + 1, upb_ref)
    finish(c, upa_ref)
    up_proj(c + 2, upa_ref)
    finish(c + 1, upb_ref)
    return carry

  assert n_chunks % 2 == 1
  lax.fori_loop(0, n_chunks // 2, chunk_pair, 0)
  finish(n_chunks - 1, upa_ref)
  y_ref[...] = _rms_norm(acc_ref[...], gfin_ref[...])


def _ffn_sample_body(x_ref, ga_ref, gb_ref, o_ref, wao_ref, wo_ref, gffn_ref, wup_ref, wfdw_ref, bfdw_ref,
                     wdn_ref, gfin_ref, hist_ref, y_ref, state_ref):
  x1 = _merge_branches(x_ref, ga_ref, gb_ref, o_ref, wao_ref, wo_ref)
  h2 = _rms_norm(x1, gffn_ref[...]).astype(BF16)
  acc = x1
  for c in range(D_FF // FF_CHUNK):
    halves = []
    for half in range(2):
      cols = slice(half * D_FF + c * FF_CHUNK, half * D_FF + (c + 1) * FF_CHUNK)
      up = jnp.dot(h2, wup_ref[c, half], preferred_element_type=F32)
      prev2, prev1 = hist_ref[:, 0, cols], hist_ref[:, 1, cols]
      state_ref[:, 0, cols] = prev1
      state_ref[:, 1, cols] = up
      taps = wfdw_ref[c, half]
      halves.append(taps[0:1] * prev2 + taps[1:2] * prev1 + taps[2:3] * up + bfdw_ref[c, half])
    acc = acc + jnp.dot(_gated(*halves), wdn_ref[c], preferred_element_type=F32)
  y_ref[...] = _rms_norm(acc, gfin_ref[...])


def _ffn_weight_specs():
  n_chunks = D_FF // FF_CHUNK
  return [_full((GROUP_WIDTH, D_MODEL)), _full((D_MODEL, D_MODEL)), _full((1, D_MODEL)),
          _full((n_chunks, 2, D_MODEL, FF_CHUNK)), _full((n_chunks, 2, FFN_CONV_WIDTH, FF_CHUNK)),
          _full((n_chunks, 2, 1, FF_CHUNK)), _full((n_chunks, FF_CHUNK, D_MODEL)), _full((1, D_MODEL))]


def _chunk_major(a):
  lead = a.shape[:-1]
  a = a.reshape(lead + (2, D_FF // FF_CHUNK, FF_CHUNK))
  nl = len(lead)
  return jnp.transpose(a, (nl + 1, nl) + tuple(range(nl)) + (nl + 2,))


def _ffn_prompt(x2, ga2, gate2, o2, weights, n_seq, t_len):
  tm = ROW_TILE
  nt = t_len // tm
  m = n_seq * t_len
  row = lambda i: (i, 0)
  n_chunks = D_FF // FF_CHUNK
  state_shape = (n_chunks, 2, FFN_CONV_WIDTH - 1, FF_CHUNK)
  up_slot = pltpu.VMEM((2, FF_CHUNK // LANES, FFN_HALO + tm, LANES), F32)
  y, state = pl.pallas_call(
      functools.partial(_ffn_prompt_body, tm=tm, nt=nt),
      grid=(m // tm,),
      in_specs=[pl.BlockSpec((tm, D_MODEL), row), pl.BlockSpec((tm, D_MODEL), row),
                pl.BlockSpec((tm, D_MODEL), row), pl.BlockSpec((tm, GROUP_WIDTH), row)]
      + _ffn_weight_specs(),
      out_specs=[pl.BlockSpec((tm, D_MODEL), row),
                 pl.BlockSpec((None,) + state_shape, lambda i: (i // nt, 0, 0, 0, 0))],
      out_shape=[jax.ShapeDtypeStruct((m, D_MODEL), F32),
                 jax.ShapeDtypeStruct((n_seq,) + state_shape, F32)],
      scratch_shapes=[pltpu.VMEM((tm, D_MODEL), BF16), pltpu.VMEM((tm, D_MODEL), F32), up_slot, up_slot,
                      pltpu.VMEM((n_chunks, 2, FF_CHUNK // LANES, FFN_HALO, LANES), F32)],
      compiler_params=_params(1), name="ffn_prompt",
  )(x2, ga2, gate2, o2, *weights)
  return y, jnp.transpose(state, (0, 3, 2, 1, 4)).reshape(n_seq, FFN_CONV_WIDTH - 1, 2 * D_FF)


def _ffn_sample(x2, ga2, gate2, o2, weights, hist):
  n = x2.shape[0]
  out_shape = [jax.ShapeDtypeStruct((n, D_MODEL), F32),
               jax.ShapeDtypeStruct((n, FFN_CONV_WIDTH - 1, 2 * D_FF), F32)]
  return pl.pallas_call(
      _ffn_sample_body,
      grid=(1,),
      in_specs=[_full((n, D_MODEL)), _full((n, D_MODEL)), pl.BlockSpec((n, D_MODEL), lambda i: (0, 1)),
                _full((n, GROUP_WIDTH))] + _ffn_weight_specs()
      + [_full((n, FFN_CONV_WIDTH - 1, 2 * D_FF))],
      out_specs=[_full(s.shape) for s in out_shape], out_shape=out_shape,
      compiler_params=_params(1), name="ffn_sample",
  )(x2, ga2, gate2, o2, *weights, hist)


def kernel(x_prompt, x_sample, state_conv, cache_k_w128, cache_v_w128, cache_k_w512, cache_v_w512,
           cache_k_w2048, cache_v_w2048, state_ffn_conv, g_mix, w_in, w_dw, b_dw, ln_g, ln_b,
           w_conv_out, w_attn_out, w_out, g_ffn, w_up, w_fdw, b_fdw, w_down, g_final):
  n_p, t_p, _ = x_prompt.shape
  n_s, t_s, _ = x_sample.shape
  assert g_mix.shape[0] == 1, "one layer"
  assert t_s == 1
  assert t_p % ATTN_TILE == 0 and ATTN_TILE % ROW_TILE == 0
  caches = ((cache_k_w128[0], cache_v_w128[0]), (cache_k_w512[0], cache_v_w512[0]),
            (cache_k_w2048[0], cache_v_w2048[0]))
  for (window, _), (kc, vc) in zip(GROUPS, caches):
    assert kc.shape == (n_s, window, HEADS_PER_GROUP, HEAD_DIM) and vc.shape == kc.shape

  row = lambda a: a.reshape(1, -1)
  w_in_b = w_in[0].astype(BF16)
  w_co_b = w_conv_out[0].astype(BF16)
  ffn_weights = (w_attn_out[0].astype(BF16), w_out[0].astype(BF16), row(g_ffn[0]),
                 _chunk_major(w_up[0].astype(BF16)), _chunk_major(w_fdw[0]), _chunk_major(row(b_fdw[0])),
                 w_down[0].astype(BF16).reshape(D_FF // FF_CHUNK, FF_CHUNK, D_MODEL), row(g_final))
  conv_weights = (w_dw[0], row(b_dw[0]), row(ln_g[0]), row(ln_b[0]), w_co_b)

  xp2 = x_prompt.reshape(n_p * t_p, D_MODEL)
  outs = _inproj_prompt(xp2, row(g_mix[0]), w_in_b, _rope_tables(jnp.arange(t_p, dtype=jnp.int32)),
                        conv_weights, n_p, t_p)
  ga_p, gb_p, qkvs = outs[0], outs[1], outs[2:2 + N_GROUPS]
  tails, conv_p = outs[2 + N_GROUPS:2 + 3 * N_GROUPS], outs[2 + 3 * N_GROUPS]
  o_p = _attn_prompt(qkvs, n_p, t_p)
  y_p, ffn_p = _ffn_prompt(xp2, ga_p, gb_p, o_p, ffn_weights, n_p, t_p)

  xs2 = x_sample.reshape(n_s, D_MODEL)
  pos_s = jnp.full((n_s,), PAST_LEN, jnp.int32)
  u_s, gate_s, qkv_blocks = _inproj_sample(xs2, row(g_mix[0]), w_in_b, _rope_tables(pos_s))
  qkv_s = jnp.transpose(qkv_blocks, (1, 0, 2)).reshape(n_s, 3 * ATTN_WIDTH)
  ga_s, conv_s = _conv_sample(u_s, gate_s, state_conv[0], *conv_weights)
  cols = lambda a: a.reshape(n_s, N_GROUPS, GROUP_WIDTH, 1)
  lanes_t = lambda c: jnp.transpose(c, (0, 2, 3, 1)).reshape(n_s, GROUP_WIDTH, c.shape[1])
  rows_t = lambda c: jnp.transpose(c.reshape(n_s, HEADS_PER_GROUP, HEAD_DIM, c.shape[2]), (0, 3, 1, 2))
  res = _sample_attn(cols(qkv_s[:, :ATTN_WIDTH]), cols(qkv_s[:, ATTN_WIDTH:2 * ATTN_WIDTH]),
                     cols(qkv_s[:, 2 * ATTN_WIDTH:]),
                     [(lanes_t(kc), lanes_t(vc)) for kc, vc in caches])
  o_s, rolled = res[0], res[1:]
  y_s, ffn_s = _ffn_sample(xs2, ga_s, gate_s, o_s.reshape(n_s, GROUP_WIDTH).astype(BF16), ffn_weights,
                           state_ffn_conv[0])

  kv_shape = lambda a: a.reshape(1, a.shape[0], a.shape[1], HEADS_PER_GROUP, HEAD_DIM)
  kv_p = [kv_shape(t) for t in tails]
  kv_s = [rows_t(r)[None] for r in rolled]
  return (y_p.reshape(n_p, t_p, D_MODEL), y_s.reshape(n_s, t_s, D_MODEL),
          conv_p[None], conv_s[None],
          kv_p[0], kv_p[1], kv_s[0], kv_s[1],
          kv_p[2], kv_p[3], kv_s[2], kv_s[3],
          kv_p[4], kv_p[5], kv_s[4], kv_s[5],
          ffn_p[None], ffn_s[None])
```
